```python
import jax, jax.numpy as jnp
from jax import lax
import numpy as np

D_MODEL = 1024
BATCH = 8
SEQ = 4096
DEPTH = 4
DEC_BATCH = 16
DEC_SEQ = 2048
PAST_LEN = 128

N_MIXERS = 2
N_A = (DEPTH + 1) // 2
N_B = DEPTH // 2
CHUNK = 64
EPS = 1e-6

MLSTM_HEADS = 4
MLSTM_QK = D_MODEL // 2
MLSTM_V = D_MODEL
MLSTM_DK = MLSTM_QK // MLSTM_HEADS
MLSTM_DV = MLSTM_V // MLSTM_HEADS
MLSTM_IN = 2 * MLSTM_QK + 2 * MLSTM_V + 4 * MLSTM_HEADS
M_INIT = -1e30

GLA_HEADS = 4
GLA_K = D_MODEL // 2
GLA_V = D_MODEL
GLA_DK = GLA_K // GLA_HEADS
GLA_DV = GLA_V // GLA_HEADS
GLA_RANK = 16
GLA_TAU = 16.0
GLA_IN = 2 * GLA_K + 2 * GLA_V + 2 * GLA_RANK

D_FF = ((8 * D_MODEL // 3 + 255) // 256) * 256

kernel_name = "bidir_mlstm_gla_hybrid_encoder"


def _rmsnorm(x, g):
    xf = x.astype(jnp.float32)
    y = xf * lax.rsqrt(jnp.mean(xf * xf, axis=-1, keepdims=True) + EPS)
    return (y * g.astype(jnp.float32)).astype(x.dtype)


def _heads(a, n_heads):
    b, t, _ = a.shape
    return a.reshape(b, t, n_heads, -1).transpose(0, 2, 1, 3)


def _merge_heads(a):
    b, n, t, d = a.shape
    return a.transpose(0, 2, 1, 3).reshape(b, t, n * d)


def _to_chunks(a):
    b, h, t = a.shape[:3]
    a = a.reshape((b, h, t // CHUNK, CHUNK) + a.shape[3:])
    return jnp.moveaxis(a, 2, 0)


def _from_chunks(a):
    a = jnp.moveaxis(a, 0, 2)
    return a.reshape(a.shape[:2] + (a.shape[2] * a.shape[3],) + a.shape[4:])


def _flip(a):
    return jnp.flip(a, axis=2)


def _mlstm_scan(q, k, v, ig, lf):
    bsz, nh, _, dk = q.shape
    dv = v.shape[-1]
    causal = jnp.tril(jnp.ones((CHUNK, CHUNK), dtype=bool))

    def step(carry, inp):
        C, n, m = carry
        qc, kc, vc, ic, fc = inp
        b = jnp.cumsum(fc, axis=-1)
        dmat = jnp.where(causal, b[..., :, None] - b[..., None, :] + ic[..., None, :], -jnp.inf)
        m_inter = b + m[..., None]
        m_t = jnp.maximum(m_inter, jnp.max(dmat, axis=-1))
        w_inter = jnp.exp(m_inter - m_t)
        s = jnp.einsum('bhtd,bhjd->bhtj', qc, kc) * jnp.exp(dmat - m_t[..., None])
        num = (w_inter[..., None] * jnp.einsum('bhtd,bhdv->bhtv', qc, C)
               + jnp.einsum('bhtj,bhjv->bhtv', s, vc))
        den = w_inter * jnp.einsum('bhtd,bhd->bht', qc, n) + jnp.sum(s, axis=-1)
        hc = num / jnp.maximum(jnp.abs(den), jnp.exp(-m_t))[..., None]
        g_last = b[..., -1]
        wk_log = g_last[..., None] - b + ic
        m_new = jnp.maximum(g_last + m, jnp.max(wk_log, axis=-1))
        w_c = jnp.exp(g_last + m - m_new)
        kw = kc * jnp.exp(wk_log - m_new[..., None])[..., None]
        C = w_c[..., None, None] * C + jnp.einsum('bhjd,bhjv->bhdv', kw, vc)
        n = w_c[..., None] * n + jnp.sum(kw, axis=-2)
        return (C, n, m_new), hc

    init = (jnp.zeros((bsz, nh, dk, dv), jnp.float32),
            jnp.zeros((bsz, nh, dk), jnp.float32),
            jnp.full((bsz, nh), M_INIT, jnp.float32))
    _, hs = lax.scan(step, init, (_to_chunks(q), _to_chunks(k), _to_chunks(v),
                                  _to_chunks(ig), _to_chunks(lf)))
    return _from_chunks(hs)


def _gla_scan(q, k, v, lg):
    bsz, nh, _, dk = q.shape
    dv = v.shape[-1]
    causal = jnp.tril(jnp.ones((CHUNK, CHUNK), dtype=bool))

    def step(S, inp):
        qc, kc, vc, gc = inp
        bc = jnp.cumsum(gc, axis=-2)
        qe = qc * jnp.exp(bc)
        ke = kc * jnp.exp(-bc)
        a = jnp.where(causal, jnp.einsum('bhtd,bhjd->bhtj', qe, ke), 0.0)
        o = jnp.einsum('bhtd,bhdv->bhtv', qe, S) + jnp.einsum('bhtj,bhjv->bhtv', a, vc)
        b_last = bc[..., -1, :]
        kd = kc * jnp.exp(b_last[..., None, :] - bc)
        S = jnp.exp(b_last)[..., None] * S + jnp.einsum('bhjd,bhjv->bhdv', kd, vc)
        return S, o

    init = jnp.zeros((bsz, nh, dk, dv), jnp.float32)
    _, os_ = lax.scan(step, init, (_to_chunks(q), _to_chunks(k), _to_chunks(v), _to_chunks(lg)))
    return _from_chunks(os_)


def _mlstm_mixer(h, w_in, b_gate, head_g, w_out):
    bsz, t, _ = h.shape
    f32 = jnp.float32
    proj = h @ w_in
    q, k, v, o, gt = jnp.split(proj, [MLSTM_QK, 2 * MLSTM_QK, 2 * MLSTM_QK + MLSTM_V,
                                      2 * MLSTM_QK + 2 * MLSTM_V], axis=-1)
    q = _heads(q, MLSTM_HEADS).astype(f32) * (MLSTM_DK ** -0.5)
    k = _heads(k, MLSTM_HEADS).astype(f32)
    v = _heads(v, MLSTM_HEADS).astype(f32)
    gt = (gt + b_gate).astype(f32).reshape(bsz, t, 4, MLSTM_HEADS).transpose(2, 0, 3, 1)
    ig_f, fg_f, ig_b, fg_b = gt[0], gt[1], gt[2], gt[3]
    h_f = _mlstm_scan(q, k, v, ig_f, jax.nn.log_sigmoid(fg_f))
    h_b = _flip(_mlstm_scan(_flip(q), _flip(k), _flip(v), _flip(ig_b),
                            _flip(jax.nn.log_sigmoid(fg_b))))
    hs = _rmsnorm(h_f + h_b, head_g.reshape(MLSTM_HEADS, 1, MLSTM_DV))
    y = _merge_heads(hs) * jax.nn.sigmoid(o.astype(f32))
    return y.astype(h.dtype) @ w_out


def _gla_mixer(h, w_in, w_a2, b_a, head_g, w_out):
    f32 = jnp.float32
    proj = h @ w_in
    q, k, v, r, a_f, a_b = jnp.split(proj, [GLA_K, 2 * GLA_K, 2 * GLA_K + GLA_V,
                                            2 * GLA_K + 2 * GLA_V,
                                            2 * GLA_K + 2 * GLA_V + GLA_RANK], axis=-1)
    q = _heads(q, GLA_HEADS).astype(f32) * (GLA_DK ** -0.5)
    k = _heads(k, GLA_HEADS).astype(f32)
    v = _heads(v, GLA_HEADS).astype(f32)
    lg_f = jax.nn.log_sigmoid((a_f @ w_a2[0] + b_a[0]).astype(f32)) / GLA_TAU
    lg_b = jax.nn.log_sigmoid((a_b @ w_a2[1] + b_a[1]).astype(f32)) / GLA_TAU
    lg_f = _heads(lg_f, GLA_HEADS)
    lg_b = _heads(lg_b, GLA_HEADS)
    o_f = _gla_scan(q, k, v, lg_f)
    o_b = _flip(_gla_scan(_flip(q), _flip(k), _flip(v), _flip(lg_b)))
    os_ = _rmsnorm(o_f + o_b, head_g.reshape(GLA_HEADS, 1, GLA_DV))
    y = _merge_heads(os_) * jax.nn.silu(r.astype(f32))
    return y.astype(h.dtype) @ w_out


def _swiglu(h, w_gu, w_down):
    g, u = jnp.split(h @ w_gu, 2, axis=-1)
    return (jax.nn.silu(g) * u) @ w_down


def _trunk(x, norm_mix_g, norm_ffn_g, norm_final_g,
           mlstm_w_in, mlstm_b_gate, mlstm_head_g, mlstm_w_out,
           gla_w_in, gla_w_a2, gla_b_a, gla_head_g, gla_w_out,
           ffn_w_gu, ffn_w_down):
    for i in range(DEPTH):
        h = _rmsnorm(x, norm_mix_g[i])
        j = i // N_MIXERS
        if i % N_MIXERS == 0:
            x = x + _mlstm_mixer(h, mlstm_w_in[j], mlstm_b_gate[j], mlstm_head_g[j], mlstm_w_out[j])
        else:
            x = x + _gla_mixer(h, gla_w_in[j], gla_w_a2[j], gla_b_a[j], gla_head_g[j], gla_w_out[j])
        h = _rmsnorm(x, norm_ffn_g[i])
        x = x + _swiglu(h, ffn_w_gu[i], ffn_w_down[i])
    return _rmsnorm(x, norm_final_g)


def setup_inputs(seed: int = 0) -> dict:
    key = jax.random.key(seed)
    ks = jax.random.split(key, 20)
    f32 = jnp.float32
    nrm = lambda k, s, sc: jax.random.normal(k, s, f32) * sc
    gate_offset = jnp.repeat(jnp.array([0.0, 3.0, 0.0, 3.0], f32), MLSTM_HEADS)
    return {
        "x_prompt": nrm(ks[0], (BATCH, SEQ, D_MODEL), 1.0),
        "x_sample": nrm(ks[1], (DEC_BATCH, DEC_SEQ, D_MODEL), 1.0),
        "norm_mix_g": 1.0 + nrm(ks[2], (DEPTH, D_MODEL), 0.02),
        "norm_ffn_g": 1.0 + nrm(ks[3], (DEPTH, D_MODEL), 0.02),
        "norm_final_g": 1.0 + nrm(ks[4], (D_MODEL,), 0.02),
        "mlstm_w_in": nrm(ks[5], (N_A, D_MODEL, MLSTM_IN), D_MODEL ** -0.5),
        "mlstm_b_gate": gate_offset + nrm(ks[6], (N_A, 4 * MLSTM_HEADS), 0.1),
        "mlstm_head_g": 1.0 + nrm(ks[7], (N_A, MLSTM_V), 0.02),
        "mlstm_w_out": nrm(ks[8], (N_A, MLSTM_V, D_MODEL), MLSTM_V ** -0.5),
        "gla_w_in": nrm(ks[9], (N_B, D_MODEL, GLA_IN), D_MODEL ** -0.5),
        "gla_w_a2": nrm(ks[10], (N_B, 2, GLA_RANK, GLA_K), GLA_RANK ** -0.5),
        "gla_b_a": nrm(ks[11], (N_B, 2, GLA_K), 0.1),
        "gla_head_g": 1.0 + nrm(ks[12], (N_B, GLA_V), 0.02),
        "gla_w_out": nrm(ks[13], (N_B, GLA_V, D_MODEL), GLA_V ** -0.5),
        "ffn_w_gu": nrm(ks[14], (DEPTH, D_MODEL, 2 * D_FF), D_MODEL ** -0.5),
        "ffn_w_down": nrm(ks[15], (DEPTH, D_FF, D_MODEL), D_FF ** -0.5),
    }


def reference(x_prompt, x_sample, norm_mix_g, norm_ffn_g, norm_final_g,
              mlstm_w_in, mlstm_b_gate, mlstm_head_g, mlstm_w_out,
              gla_w_in, gla_w_a2, gla_b_a, gla_head_g, gla_w_out,
              ffn_w_gu, ffn_w_down):
    y_prompt = _trunk(x_prompt, norm_mix_g, norm_ffn_g, norm_final_g,
                      mlstm_w_in, mlstm_b_gate, mlstm_head_g, mlstm_w_out,
                      gla_w_in, gla_w_a2, gla_b_a, gla_head_g, gla_w_out,
                      ffn_w_gu, ffn_w_down)
    y_sample = _trunk(x_sample, norm_mix_g, norm_ffn_g, norm_final_g,
                      mlstm_w_in, mlstm_b_gate, mlstm_head_g, mlstm_w_out,
                      gla_w_in, gla_w_a2, gla_b_a, gla_head_g, gla_w_out,
                      ffn_w_gu, ffn_w_down)
    return (y_prompt, y_sample)
```

```python
import functools

import jax
import jax.numpy as jnp
from jax import lax
from jax.experimental import pallas as pl
from jax.experimental.pallas import tpu as pltpu

F32 = jnp.float32
BF16 = jnp.bfloat16

EPS = 1e-6
HEADS = 4
DK = 128
DV = 256
GLA_RANK = 16
GLA_TAU = 16.0
NEG = -1e30

MLSTM_CHUNK = 128
GLA_CHUNK = 64
TOKEN_TILE = 512
FFN_CHUNK = 256
VMEM_LIMIT = 56 * 1024 * 1024


def _params(n_parallel):
    return pltpu.CompilerParams(
        dimension_semantics=("parallel",) * n_parallel,
        vmem_limit_bytes=VMEM_LIMIT)


def _rmsnorm(x, g):
    return x * lax.rsqrt(jnp.mean(x * x, axis=-1, keepdims=True) + EPS) * g


def _log_sigmoid(x):
    return jnp.minimum(x, 0.0) - jnp.log(1.0 + jnp.exp(-jnp.abs(x)))


def _dot(a, b):
    return jnp.dot(a, b, preferred_element_type=F32)


def _dot_nt(a, b):
    return lax.dot_general(a, b, (((1,), (1,)), ((), ())), preferred_element_type=F32)


def _dot_tn(a, b):
    return lax.dot_general(a, b, (((0,), (0,)), ((), ())), preferred_element_type=F32)


def _inproj_kernel(x_ref, g_ref, *refs, scales):
    n = len(scales)
    w_refs, o_refs = refs[:n], refs[n:]
    h = _rmsnorm(x_ref[...], g_ref[...]).astype(BF16)
    for w_ref, o_ref, s in zip(w_refs, o_refs, scales):
        y = _dot(h, w_ref[...])
        if s != 1.0:
            y = y * s
        o_ref[...] = y.astype(o_ref.dtype)


def _inproj(x, g, weights, scales, dtypes):
    n_tok, d = x.shape
    tm = TOKEN_TILE
    const = lambda i: (0, 0)
    return pl.pallas_call(
        functools.partial(_inproj_kernel, scales=tuple(scales)),
        grid=(n_tok // tm,),
        in_specs=[pl.BlockSpec((tm, d), lambda i: (i, 0)),
                  pl.BlockSpec((1, d), const)]
                 + [pl.BlockSpec(w.shape, const) for w in weights],
        out_specs=[pl.BlockSpec((tm, w.shape[1]), lambda i: (i, 0)) for w in weights],
        out_shape=[jax.ShapeDtypeStruct((n_tok, w.shape[1]), dt)
                   for w, dt in zip(weights, dtypes)],
        compiler_params=_params(1),
        name="inproj",
    )(x, g.reshape(1, d), *weights)


def _scan_lanes(x, op, fill, reverse):
    t = x.shape[-1]
    lane = lax.broadcasted_iota(jnp.int32, x.shape, 1)
    s = 1
    while s < t:
        if reverse:
            x = op(x, jnp.where(lane < t - s, pltpu.roll(x, t - s, axis=1), fill))
        else:
            x = op(x, jnp.where(lane >= s, pltpu.roll(x, s, axis=1), fill))
        s *= 2
    return x


def _mlstm_gates_kernel(gi_ref, gf_ref, bi_ref, bf_ref, out_ref):
    lf = _log_sigmoid(gf_ref[0] + bf_ref[...])
    fwd = lax.broadcasted_iota(jnp.int32, lf.shape, 0) < HEADS
    b = jnp.where(fwd, _scan_lanes(lf, jnp.add, 0.0, False),
                  _scan_lanes(lf, jnp.add, 0.0, True))
    a = gi_ref[0] + bi_ref[...] - b
    amax = jnp.where(fwd, _scan_lanes(a, jnp.maximum, NEG, False),
                     _scan_lanes(a, jnp.maximum, NEG, True))
    out_ref[0, 0] = a
    out_ref[0, 1] = amax
    out_ref[0, 2] = b + amax


def _mlstm_gates(gi, gf, bi, bf):
    bsz, _, t = gi.shape
    blk = pl.BlockSpec((1, 2 * HEADS, t), lambda b: (b, 0, 0))
    vec = pl.BlockSpec((2 * HEADS, 1), lambda b: (0, 0))
    return pl.pallas_call(
        _mlstm_gates_kernel,
        grid=(bsz,),
        in_specs=[blk, blk, vec, vec],
        out_specs=pl.BlockSpec((1, 3, 2 * HEADS, t), lambda b: (b, 0, 0, 0)),
        out_shape=jax.ShapeDtypeStruct((bsz, 3, 2 * HEADS, t), F32),
        compiler_params=_params(1),
        name="mlstm_gates",
    )(gi, gf, bi, bf)


def _gated_head_output(hf_ref, hb_ref, gate_ref, hg_ref, y_ref, gate_fn, rows):
    t = hf_ref.shape[0]

    def body(i, carry):
        s = pl.multiple_of(i * rows, rows)
        hs = hf_ref[pl.ds(s, rows), :] + hb_ref[pl.ds(s, rows), :]
        y = _rmsnorm(hs, hg_ref[0]) * gate_fn(gate_ref[0, pl.ds(s, rows), :].astype(F32))
        y_ref[0, pl.ds(s, rows), :] = y.astype(y_ref.dtype)
        return carry

    lax.fori_loop(0, t // rows, body, 0)


def _mlstm_scan_kernel(q_ref, k_ref, v_ref, o_ref, row_ref, col_ref, hg_ref, y_ref,
                       hf_ref, hb_ref, c_ref, n_ref, *, chunk):
    t = q_ref.shape[1]
    nc = t // chunk
    c_ref[...] = jnp.zeros_like(c_ref)
    n_ref[...] = jnp.zeros_like(n_ref)
    ti = lax.broadcasted_iota(jnp.int32, (chunk, chunk), 0)
    ji = lax.broadcasted_iota(jnp.int32, (chunk, chunk), 1)
    masks = (ji <= ti, ji >= ti)
    h_refs = (hf_ref, hb_ref)

    def step(c, d, a_prev):
        s = pl.multiple_of(c * chunk, chunk)
        qc = q_ref[0, pl.ds(s, chunk), :]
        kc = k_ref[0, pl.ds(s, chunk), :]
        vc = v_ref[0, pl.ds(s, chunk), :]
        a_row = row_ref[0, 0, d:d + 1, pl.ds(s, chunk)]
        cols = col_ref[0, 0, pl.ds(s, chunk), :]
        a_col, amax, m_col = cols[:, d:d + 1], cols[:, 2 + d:3 + d], cols[:, 4 + d:5 + d]
        edge = chunk - 1 if d == 0 else 0
        a_end = amax[edge:edge + 1, :]

        w = jnp.exp(jnp.where(masks[d], a_row - amax, NEG))
        p = _dot_nt(qc, kc) * w
        scale = jnp.exp(a_prev - amax)
        c_old, n_old = c_ref[d], n_ref[d]
        num = scale * _dot(qc, c_old.astype(BF16)) + _dot(p.astype(BF16), vc)
        den = (scale * jnp.sum(qc.astype(F32) * n_old, axis=1, keepdims=True)
               + jnp.sum(p, axis=1, keepdims=True))
        h_refs[d][pl.ds(s, chunk), :] = num * (1.0 / jnp.maximum(jnp.abs(den), jnp.exp(-m_col)))

        kw = kc.astype(F32) * jnp.exp(a_col - a_end)
        decay = jnp.exp(a_prev - a_end)
        c_ref[d] = decay * c_old + _dot_tn(kw.astype(BF16), vc)
        n_ref[d] = decay * n_old + jnp.sum(kw, axis=0, keepdims=True)
        return a_end

    def body(i, carry):
        return step(i, 0, carry[0]), step(nc - 1 - i, 1, carry[1])

    empty = jnp.full((1, 1), NEG, F32)
    lax.fori_loop(0, nc, body, (empty, empty))
    _gated_head_output(hf_ref, hb_ref, o_ref, hg_ref, y_ref, jax.nn.sigmoid, 256)


def _mlstm_scan(q, k, v, o, rows, cols, head_g):
    bsz, t, _ = q.shape
    qk_spec = pl.BlockSpec((1, t, DK), lambda b, h: (b, 0, h))
    v_spec = pl.BlockSpec((1, t, DV), lambda b, h: (b, 0, h))
    return pl.pallas_call(
        functools.partial(_mlstm_scan_kernel, chunk=MLSTM_CHUNK),
        grid=(bsz, HEADS),
        in_specs=[qk_spec, qk_spec, v_spec, v_spec,
                  pl.BlockSpec((1, 1, 6, t), lambda b, h: (b, h, 0, 0)),
                  pl.BlockSpec((1, 1, t, 6), lambda b, h: (b, h, 0, 0)),
                  pl.BlockSpec((1, 1, DV), lambda b, h: (h, 0, 0))],
        out_specs=v_spec,
        out_shape=jax.ShapeDtypeStruct((bsz, t, HEADS * DV), BF16),
        scratch_shapes=[pltpu.VMEM((t, DV), F32), pltpu.VMEM((t, DV), F32),
                        pltpu.VMEM((2, DK, DV), F32), pltpu.VMEM((2, 1, DK), F32)],
        compiler_params=_params(2),
        name="mlstm_scan",
    )(q, k, v, o, rows, cols, head_g)


def _chunk_cumsum(x, chunk, reverse):
    n = x.shape[0]
    pos = lax.broadcasted_iota(jnp.int32, x.shape, 0) % chunk
    s = 1
    while s < chunk:
        if reverse:
            x = x + jnp.where(pos < chunk - s, pltpu.roll(x, n - s, axis=0), 0.0)
        else:
            x = x + jnp.where(pos >= s, pltpu.roll(x, s, axis=0), 0.0)
        s *= 2
    return x


def _gla_scan_kernel(q_ref, k_ref, v_ref, r_ref, a_ref, w2_ref, ba_ref, hg_ref, y_ref,
                     of_ref, ob_ref, bc_ref, s_ref, *, chunk, rows):
    t = q_ref.shape[1]
    nc = t // chunk
    s_ref[...] = jnp.zeros_like(s_ref)

    def gate_body(i, carry):
        s = pl.multiple_of(i * rows, rows)
        z = _dot(a_ref[0, pl.ds(s, rows), :].astype(BF16), w2_ref[0]) + ba_ref[0]
        lg = _log_sigmoid(z) / GLA_TAU
        bc_ref[pl.ds(s, rows), :DK] = _chunk_cumsum(lg[:, :DK], chunk, False)
        bc_ref[pl.ds(s, rows), DK:] = _chunk_cumsum(lg[:, DK:], chunk, True)
        return carry

    lax.fori_loop(0, t // rows, gate_body, 0)

    ti = lax.broadcasted_iota(jnp.int32, (chunk, chunk), 0)
    ji = lax.broadcasted_iota(jnp.int32, (chunk, chunk), 1)
    masks = (ji <= ti, ji >= ti)
    eye = (lax.broadcasted_iota(jnp.int32, (DK, DK), 0)
           == lax.broadcasted_iota(jnp.int32, (DK, DK), 1))
    o_refs = (of_ref, ob_ref)

    def step(c, d):
        s = pl.multiple_of(c * chunk, chunk)
        qc = q_ref[0, pl.ds(s, chunk), :].astype(F32)
        kc = k_ref[0, pl.ds(s, chunk), :].astype(F32)
        vc = v_ref[0, pl.ds(s, chunk), :]
        bc = bc_ref[pl.ds(s, chunk), d * DK:(d + 1) * DK]
        edge = chunk - 1 if d == 0 else 0
        b_last = bc[edge:edge + 1, :]
        qe = (qc * jnp.exp(bc)).astype(BF16)
        ke = (kc * jnp.exp(-bc)).astype(BF16)
        kd = (kc * jnp.exp(b_last - bc)).astype(BF16)
        amat = jnp.where(masks[d], _dot_nt(qe, ke), 0.0)
        s_old = s_ref[d]
        o_refs[d][pl.ds(s, chunk), :] = _dot(qe, s_old.astype(BF16)) + _dot(amat.astype(BF16), vc)
        e_col = jnp.sum(jnp.where(eye, jnp.exp(b_last), 0.0), axis=1, keepdims=True)
        s_ref[d] = e_col * s_old + _dot_tn(kd, vc)

    def body(i, carry):
        step(i, 0)
        step(nc - 1 - i, 1)
        return carry

    lax.fori_loop(0, nc, body, 0)
    _gated_head_output(of_ref, ob_ref, r_ref, hg_ref, y_ref, jax.nn.silu, rows)


def _gla_scan(q, k, v, r, a, w2, ba, head_g):
    bsz, t, _ = q.shape
    qk_spec = pl.BlockSpec((1, t, DK), lambda b, h: (b, 0, h))
    v_spec = pl.BlockSpec((1, t, DV), lambda b, h: (b, 0, h))
    return pl.pallas_call(
        functools.partial(_gla_scan_kernel, chunk=GLA_CHUNK, rows=256),
        grid=(bsz, HEADS),
        in_specs=[qk_spec, qk_spec, v_spec, v_spec,
                  pl.BlockSpec((1, t, 2 * GLA_RANK), lambda b, h: (b, 0, 0)),
                  pl.BlockSpec((1, 2 * GLA_RANK, 2 * DK), lambda b, h: (h, 0, 0)),
                  pl.BlockSpec((1, 1, 2 * DK), lambda b, h: (h, 0, 0)),
                  pl.BlockSpec((1, 1, DV), lambda b, h: (h, 0, 0))],
        out_specs=v_spec,
        out_shape=jax.ShapeDtypeStruct((bsz, t, HEADS * DV), BF16),
        scratch_shapes=[pltpu.VMEM((t, DV), F32), pltpu.VMEM((t, DV), F32),
                        pltpu.VMEM((t, 2 * DK), F32), pltpu.VMEM((2, DK, DV), F32)],
        compiler_params=_params(2),
        name="gla_scan",
    )(q, k, v, r, a, w2, ba, head_g)


def _mix_ffn_kernel(y_ref, x_ref, wo_ref, gn_ref, wg_ref, wu_ref, wd_ref, gl_ref, out_ref,
                    *, final):
    x1 = x_ref[...] + _dot(y_ref[...], wo_ref[...])
    h = _rmsnorm(x1, gn_ref[...]).astype(BF16)
    acc = x1
    d_ff = wg_ref.shape[1]
    for c in range(d_ff // FFN_CHUNK):
        cs = slice(c * FFN_CHUNK, (c + 1) * FFN_CHUNK)
        g = _dot(h, wg_ref[:, cs])
        u = _dot(h, wu_ref[:, cs])
        acc = acc + _dot((jax.nn.silu(g) * u).astype(BF16), wd_ref[cs, :])
    if final:
        acc = _rmsnorm(acc, gl_ref[...])
    out_ref[...] = acc


def _mix_ffn(y, x, wo, gn, wg, wu, wd, gl, final):
    n_tok, d = x.shape
    tm = TOKEN_TILE
    const = lambda i: (0, 0)
    resident = lambda w: pl.BlockSpec(w.shape, const, pipeline_mode=pl.Buffered(1))
    tile = pl.BlockSpec((tm, d), lambda i: (i, 0))
    vec = pl.BlockSpec((1, d), const)
    return pl.pallas_call(
        functools.partial(_mix_ffn_kernel, final=final),
        grid=(n_tok // tm,),
        in_specs=[tile, tile, resident(wo), vec, resident(wg), resident(wu), resident(wd), vec],
        out_specs=tile,
        out_shape=jax.ShapeDtypeStruct((n_tok, d), F32),
        compiler_params=_params(1),
        name="mix_ffn",
    )(y, x, wo, gn.reshape(1, d), wg, wu, wd, gl.reshape(1, d))


def _mlstm_mixer(x, bsz, t, g_norm, w, b_gate, head_g):
    qk = HEADS * DK
    vv = HEADS * DV
    q, k, v, o, gt = _inproj(x, g_norm, w, (DK ** -0.5, 1.0, 1.0, 1.0, 1.0),
                             (BF16, BF16, BF16, BF16, F32))
    gt = gt.reshape(bsz, t, 4, HEADS).transpose(0, 2, 3, 1)
    gi = jnp.concatenate([gt[:, 0], gt[:, 2]], axis=1)
    gf = jnp.concatenate([gt[:, 1], gt[:, 3]], axis=1)
    bg = b_gate.reshape(4, HEADS)
    bi = jnp.concatenate([bg[0], bg[2]]).reshape(2 * HEADS, 1)
    bf = jnp.concatenate([bg[1], bg[3]]).reshape(2 * HEADS, 1)
    stats = _mlstm_gates(gi, gf, bi, bf)
    rows = stats.reshape(bsz, 3, 2, HEADS, t).transpose(0, 3, 1, 2, 4).reshape(bsz, HEADS, 6, t)
    cols = rows.transpose(0, 1, 3, 2)
    return _mlstm_scan(q.reshape(bsz, t, qk), k.reshape(bsz, t, qk), v.reshape(bsz, t, vv),
                       o.reshape(bsz, t, vv), rows, cols, head_g.reshape(HEADS, 1, DV))


def _gla_mixer(x, bsz, t, g_norm, w, w2, ba, head_g):
    qk = HEADS * DK
    vv = HEADS * DV
    q, k, v, r, a = _inproj(x, g_norm, w, (DK ** -0.5, 1.0, 1.0, 1.0, 1.0),
                            (BF16, BF16, BF16, BF16, F32))
    return _gla_scan(q.reshape(bsz, t, qk), k.reshape(bsz, t, qk), v.reshape(bsz, t, vv),
                     r.reshape(bsz, t, vv), a.reshape(bsz, t, 2 * GLA_RANK), w2, ba,
                     head_g.reshape(HEADS, 1, DV))


def _split_cols(w, widths):
    out, s = [], 0
    for n in widths:
        out.append(w[:, s:s + n].astype(BF16))
        s += n
    return out


def _gla_gate_weights(w_a2, b_a):
    wf = w_a2[0].reshape(GLA_RANK, HEADS, DK).transpose(1, 0, 2)
    wb = w_a2[1].reshape(GLA_RANK, HEADS, DK).transpose(1, 0, 2)
    z = jnp.zeros_like(wf)
    w2 = jnp.concatenate([jnp.concatenate([wf, z], axis=2),
                          jnp.concatenate([z, wb], axis=2)], axis=1).astype(BF16)
    ba = jnp.concatenate([b_a[0].reshape(HEADS, 1, DK), b_a[1].reshape(HEADS, 1, DK)], axis=2)
    return w2, ba


def kernel(x_prompt, x_sample, norm_mix_g, norm_ffn_g, norm_final_g, mlstm_w_in, mlstm_b_gate, mlstm_head_g, mlstm_w_out, gla_w_in, gla_w_a2, gla_b_a, gla_head_g, gla_w_out, ffn_w_gu, ffn_w_down):
    depth = norm_mix_g.shape[0]
    d_ff = ffn_w_down.shape[1]
    qk, vv = HEADS * DK, HEADS * DV
    mlstm_w = [_split_cols(w, (qk, qk, vv, vv, 4 * HEADS)) for w in mlstm_w_in]
    gla_w = [_split_cols(w, (qk, qk, vv, vv, 2 * GLA_RANK)) for w in gla_w_in]
    gla_gate = [_gla_gate_weights(w, b) for w, b in zip(gla_w_a2, gla_b_a)]
    w_out = [mlstm_w_out.astype(BF16), gla_w_out.astype(BF16)]
    w_g = ffn_w_gu[:, :, :d_ff].astype(BF16)
    w_u = ffn_w_gu[:, :, d_ff:].astype(BF16)
    w_d = ffn_w_down.astype(BF16)

    outs = []
    for x0 in (x_prompt, x_sample):
        bsz, t, d = x0.shape
        x = x0.reshape(bsz * t, d)
        for i in range(depth):
            j = i // 2
            if i % 2 == 0:
                y = _mlstm_mixer(x, bsz, t, norm_mix_g[i], mlstm_w[j], mlstm_b_gate[j],
                                 mlstm_head_g[j])
            else:
                y = _gla_mixer(x, bsz, t, norm_mix_g[i], gla_w[j], *gla_gate[j], gla_head_g[j])
            x = _mix_ffn(y.reshape(bsz * t, vv), x, w_out[i % 2][j], norm_ffn_g[i],
                         w_g[i], w_u[i], w_d[i], norm_final_g, final=(i == depth - 1))
        outs.append(x.reshape(bsz, t, d))
    return tuple(outs)
```

```python
import functools

import jax
import jax.numpy as jnp
from jax import lax
from jax.experimental import pallas as pl
from jax.experimental.pallas import tpu as pltpu

F32 = jnp.float32
BF16 = jnp.bfloat16

EPS = 1e-6
HEADS = 4
DK = 128
DV = 256
LANES = 128
DVN = DV + LANES
GLA_RANK = 16
GLA_TAU = 16.0
NEG = -1e30

CHUNK = 128
GLA_SUB = 64
UNROLL_INDEPENDENT = 4
GROUP = 4
GATE_GROUP = 4
TOKEN_TILE = 512
FFN_CHUNK = 256
VMEM_LIMIT = 56 * 1024 * 1024


def _params(n_parallel):
    return pltpu.CompilerParams(
        dimension_semantics=("parallel",) * n_parallel,
        vmem_limit_bytes=VMEM_LIMIT)


def _rmsnorm(x, g):
    return x * lax.rsqrt(jnp.mean(x * x, axis=-1, keepdims=True) + EPS) * g


def _log_sigmoid(x):
    return jnp.minimum(x, 0.0) - jnp.log(1.0 + jnp.exp(-jnp.abs(x)))


def _dot(a, b):
    return jnp.dot(a, b, preferred_element_type=F32)


def _dot_nt(a, b):
    return lax.dot_general(a, b, (((1,), (1,)), ((), ())), preferred_element_type=F32)


def _dot_tn(a, b):
    return lax.dot_general(a, b, (((0,), (0,)), ((), ())), preferred_element_type=F32)


def _causal_masks(n):
    ti = lax.broadcasted_iota(jnp.int32, (n, n), 0)
    ji = lax.broadcasted_iota(jnp.int32, (n, n), 1)
    return ji <= ti, ji >= ti


def _inproj_kernel(x_ref, g_ref, *refs, scales):
    n = len(scales)
    w_refs, o_refs = refs[:n], refs[n:]
    h = _rmsnorm(x_ref[...], g_ref[...]).astype(BF16)
    for w_ref, o_ref, s in zip(w_refs, o_refs, scales):
        y = _dot(h, w_ref[...])
        if s != 1.0:
            y = y * s
        o_ref[...] = y.astype(o_ref.dtype)


def _inproj(x, g, weights, scales, dtypes):
    n_tok, d = x.shape
    tm = TOKEN_TILE
    const = lambda i: (0, 0)
    return pl.pallas_call(
        functools.partial(_inproj_kernel, scales=tuple(scales)),
        grid=(n_tok // tm,),
        in_specs=[pl.BlockSpec((tm, d), lambda i: (i, 0)),
                  pl.BlockSpec((1, d), const)]
                 + [pl.BlockSpec(w.shape, const) for w in weights],
        out_specs=[pl.BlockSpec((tm, w.shape[1]), lambda i: (i, 0)) for w in weights],
        out_shape=[jax.ShapeDtypeStruct((n_tok, w.shape[1]), dt)
                   for w, dt in zip(weights, dtypes)],
        compiler_params=_params(1),
        name="inproj",
    )(x, g.reshape(1, d), *weights)


def _scan_lanes(x, op, fill, reverse):
    t = x.shape[-1]
    lane = lax.broadcasted_iota(jnp.int32, x.shape, 1)
    s = 1
    while s < t:
        if reverse:
            x = op(x, jnp.where(lane < t - s, pltpu.roll(x, t - s, axis=1), fill))
        else:
            x = op(x, jnp.where(lane >= s, pltpu.roll(x, s, axis=1), fill))
        s *= 2
    return x


def _mlstm_gates_kernel(gi_ref, gf_ref, bi_ref, bf_ref, out_ref):
    lf = _log_sigmoid(gf_ref[0] + bf_ref[...])
    fwd = lax.broadcasted_iota(jnp.int32, lf.shape, 0) < HEADS
    b = jnp.where(fwd, _scan_lanes(lf, jnp.add, 0.0, False),
                  _scan_lanes(lf, jnp.add, 0.0, True))
    a = gi_ref[0] + bi_ref[...] - b
    amax = jnp.where(fwd, _scan_lanes(a, jnp.maximum, NEG, False),
                     _scan_lanes(a, jnp.maximum, NEG, True))
    out_ref[0, 0] = a
    out_ref[0, 1] = amax
    out_ref[0, 2] = b + amax


def _mlstm_gates(gi, gf, bi, bf):
    bsz, _, t = gi.shape
    blk = pl.BlockSpec((1, 2 * HEADS, t), lambda b: (b, 0, 0))
    vec = pl.BlockSpec((2 * HEADS, 1), lambda b: (0, 0))
    return pl.pallas_call(
        _mlstm_gates_kernel,
        grid=(bsz,),
        in_specs=[blk, blk, vec, vec],
        out_specs=pl.BlockSpec((1, 3, 2 * HEADS, t), lambda b: (b, 0, 0, 0)),
        out_shape=jax.ShapeDtypeStruct((bsz, 3, 2 * HEADS, t), F32),
        compiler_params=_params(1),
        name="mlstm_gates",
    )(gi, gf, bi, bf)


def _head_norm(hs, g):
    ss = _dot((hs * hs).astype(BF16), jnp.ones((DV, LANES), BF16))
    inv = lax.rsqrt(ss * (1.0 / DV) + EPS)
    return hs * jnp.concatenate([inv] * (DV // LANES), axis=1) * g


def _mlstm_scan_kernel(q_ref, k_ref, v_ref, o_ref, row_ref, col_ref, hg_ref, y_ref,
                       dc_ref, cb_ref, ae_ref, ap_ref):
    t = q_ref.shape[1]
    nc = t // CHUNK
    masks = _causal_masks(CHUNK)
    ones = jnp.ones((CHUNK, LANES), BF16)

    def contributions(c, carry):
        s = pl.multiple_of(c * CHUNK, CHUNK)
        kc = k_ref[0, pl.ds(s, CHUNK), :].astype(F32)
        va = jnp.concatenate([v_ref[0, pl.ds(s, CHUNK), :], ones], axis=1)
        cols = col_ref[0, 0, pl.ds(s, CHUNK), :]
        for d in (0, 1):
            edge = CHUNK - 1 if d == 0 else 0
            a_end = cols[edge:edge + 1, 2 + d:3 + d]
            a_bc = jnp.broadcast_to(cols[:, d:d + 1], (CHUNK, DK))
            kw = kc * jnp.exp(a_bc - a_end)
            dc_ref[d, c] = _dot_tn(kw.astype(BF16), va)
            ae_ref[d, c] = jnp.broadcast_to(a_end, (8, LANES))
        return carry

    lax.fori_loop(0, nc, contributions, 0, unroll=UNROLL_INDEPENDENT)

    def recurrence(d):
        def body(i, carry):
            cst, a_prev = carry
            c = i if d == 0 else nc - 1 - i
            cb_ref[d, c] = cst.astype(BF16)
            ap_ref[d, c] = jnp.broadcast_to(a_prev, (8, LANES))
            a_end = ae_ref[d, c][0:1, 0:1]
            return jnp.exp(a_prev - a_end) * cst + dc_ref[d, c], a_end

        lax.fori_loop(0, nc, body, (jnp.zeros((DK, DVN), F32), jnp.full((1, 1), NEG, F32)),
                      unroll=2)

    recurrence(0)
    recurrence(1)

    def outputs(i, carry):
        group = [i * GROUP + j for j in range(GROUP)]
        starts = [pl.multiple_of(c * CHUNK, CHUNK) for c in group]
        q = [q_ref[0, pl.ds(s, CHUNK), :] for s in starts]
        sm = [_dot_nt(qc, k_ref[0, pl.ds(s, CHUNK), :]) for qc, s in zip(q, starts)]
        res, floor = [], []
        for c, s, qc, smc in zip(group, starts, q, sm):
            va = jnp.concatenate([v_ref[0, pl.ds(s, CHUNK), :], ones], axis=1)
            cols = col_ref[0, 0, pl.ds(s, CHUNK), :]
            qf = qc.astype(F32)
            for d in (0, 1):
                a_row = row_ref[0, 0, d:d + 1, pl.ds(s, CHUNK)]
                amax = jnp.broadcast_to(cols[:, 2 + d:3 + d], (CHUNK, LANES))
                m_t = jnp.broadcast_to(cols[:, 4 + d:5 + d], (CHUNK, LANES))
                a_prev = ap_ref[d, c][0:1, 0:1]
                p = smc * jnp.exp(jnp.where(masks[d], a_row - amax, NEG))
                sq = qf * jnp.exp(a_prev - amax)
                lhs = jnp.concatenate([p.astype(BF16), sq.astype(BF16)], axis=1)
                rhs = jnp.concatenate([va, cb_ref[d, c]], axis=0)
                res.append(_dot(lhs, rhs))
                floor.append(jnp.exp(-m_t))
        hs = []
        for j in range(GROUP):
            h = None
            for d in (0, 1):
                rd = res[2 * j + d]
                r = 1.0 / jnp.maximum(jnp.abs(rd[:, DV:]), floor[2 * j + d])
                hd = rd[:, :DV] * jnp.concatenate([r] * (DV // LANES), axis=1)
                h = hd if h is None else h + hd
            hs.append(h)
        normed = [_head_norm(h, hg_ref[0]) for h in hs]
        for s, y in zip(starts, normed):
            gate = jax.nn.sigmoid(o_ref[0, pl.ds(s, CHUNK), :].astype(F32))
            y_ref[0, pl.ds(s, CHUNK), :] = (y * gate).astype(y_ref.dtype)
        return carry

    lax.fori_loop(0, nc // GROUP, outputs, 0)


def _mlstm_scan(q, k, v, o, rows, cols, head_g):
    bsz, t, _ = q.shape
    assert t % (CHUNK * GROUP) == 0, t
    nc = t // CHUNK
    qk_spec = pl.BlockSpec((1, t, DK), lambda b, h: (b, 0, h))
    v_spec = pl.BlockSpec((1, t, DV), lambda b, h: (b, 0, h))
    return pl.pallas_call(
        _mlstm_scan_kernel,
        grid=(bsz, HEADS),
        in_specs=[qk_spec, qk_spec, v_spec, v_spec,
                  pl.BlockSpec((1, 1, 6, t), lambda b, h: (b, h, 0, 0)),
                  pl.BlockSpec((1, 1, t, 6), lambda b, h: (b, h, 0, 0)),
                  pl.BlockSpec((1, 1, DV), lambda b, h: (h, 0, 0))],
        out_specs=v_spec,
        out_shape=jax.ShapeDtypeStruct((bsz, t, HEADS * DV), BF16),
        scratch_shapes=[pltpu.VMEM((2, nc, DK, DVN), F32), pltpu.VMEM((2, nc, DK, DVN), BF16),
                        pltpu.VMEM((2, nc, 8, LANES), F32), pltpu.VMEM((2, nc, 8, LANES), F32)],
        compiler_params=_params(2),
        name="mlstm_scan",
    )(q, k, v, o, rows, cols, head_g)


def _gla_scan_kernel(q_ref, k_ref, v_ref, r_ref, a_ref, w2_ref, ba_ref, hg_ref, y_ref,
                     bc_ref, ds_ref, sb_ref):
    t = q_ref.shape[1]
    nc = t // CHUNK
    sub = GLA_SUB
    masks = _causal_masks(CHUNK)
    tri = tuple(m.astype(BF16) for m in masks)

    def gates(i, carry):
        rows = GATE_GROUP * CHUNK
        s = pl.multiple_of(i * rows, rows)
        z = _dot(a_ref[0, pl.ds(s, rows), :].astype(BF16), w2_ref[0]) + ba_ref[0]
        lg = _log_sigmoid(z) / GLA_TAU
        hi = lg.astype(BF16)
        lo = (lg - hi.astype(F32)).astype(BF16)
        parts = []
        for j in range(GATE_GROUP):
            rs = slice(j * CHUNK, (j + 1) * CHUNK)
            for d in (0, 1):
                cs = slice(d * DK, (d + 1) * DK)
                parts.append(_dot(tri[d], jnp.concatenate([hi[rs, cs], lo[rs, cs]], axis=1)))
        for j in range(GATE_GROUP):
            for d in (0, 1):
                pd = parts[2 * j + d]
                bc_ref[pl.ds(s + j * CHUNK, CHUNK), d * DK:(d + 1) * DK] = pd[:, :DK] + pd[:, DK:]
        return carry

    lax.fori_loop(0, nc // GATE_GROUP, gates, 0)

    def contributions(c, carry):
        s = pl.multiple_of(c * CHUNK, CHUNK)
        kc = k_ref[0, pl.ds(s, CHUNK), :].astype(F32)
        vc = v_ref[0, pl.ds(s, CHUNK), :]
        for d in (0, 1):
            bc = bc_ref[pl.ds(s, CHUNK), d * DK:(d + 1) * DK]
            edge = CHUNK - 1 if d == 0 else 0
            kd = kc * jnp.exp(bc[edge:edge + 1, :] - bc)
            ds_ref[d, c] = _dot_tn(kd.astype(BF16), vc)
        return carry

    lax.fori_loop(0, nc, contributions, 0, unroll=UNROLL_INDEPENDENT)

    eye = (lax.broadcasted_iota(jnp.int32, (DK, DK), 0)
           == lax.broadcasted_iota(jnp.int32, (DK, DK), 1))

    def recurrence(d):
        def body(i, st):
            c = i if d == 0 else nc - 1 - i
            sb_ref[d, c] = st.astype(BF16)
            tile = pl.multiple_of(c * CHUNK + (CHUNK - 8 if d == 0 else 0), 8)
            row = 7 if d == 0 else 0
            b_last = bc_ref[pl.ds(tile, 8), d * DK:(d + 1) * DK][row:row + 1, :]
            e_col = jnp.sum(jnp.where(eye, jnp.exp(b_last), 0.0), axis=1, keepdims=True)
            return e_col * st + ds_ref[d, c]

        lax.fori_loop(0, nc, body, jnp.zeros((DK, DV), F32), unroll=UNROLL_INDEPENDENT)

    recurrence(0)
    recurrence(1)

    zeros = jnp.zeros((sub, DK), F32)

    def outputs(i, carry):
        group = [i * GROUP + j for j in range(GROUP)]
        starts = [pl.multiple_of(c * CHUNK, CHUNK) for c in group]
        scores, q_state = [], []
        for s in starts:
            qc = q_ref[0, pl.ds(s, CHUNK), :].astype(F32)
            kc = k_ref[0, pl.ds(s, CHUNK), :].astype(F32)
            for d in (0, 1):
                bc = bc_ref[pl.ds(s, CHUNK), d * DK:(d + 1) * DK]
                first, second = ((slice(0, sub), slice(sub, CHUNK)) if d == 0
                                 else (slice(sub, CHUNK), slice(0, sub)))
                ref_row = sub - 1 if d == 0 else sub
                r = bc[ref_row:ref_row + 1, :]
                qe = qc * jnp.exp(bc)
                qg = qc[second] * jnp.exp(bc[second] - r)
                kf = kc[first] * jnp.exp(-bc[first])
                kg = kc * jnp.exp(r - bc)
                q_first = jnp.concatenate([qe[first], zeros], axis=1)
                q_second = jnp.concatenate([zeros, qg], axis=1)
                if d == 0:
                    lhs = jnp.concatenate([q_first, q_second], axis=0)
                    kf_all = jnp.concatenate([kf, zeros], axis=0)
                else:
                    lhs = jnp.concatenate([q_second, q_first], axis=0)
                    kf_all = jnp.concatenate([zeros, kf], axis=0)
                keys = jnp.concatenate([kf_all, kg], axis=1)
                scores.append(_dot_nt(lhs.astype(BF16), keys.astype(BF16)))
                q_state.append(qe.astype(BF16))
        outs = []
        for j, (c, s) in enumerate(zip(group, starts)):
            vc = v_ref[0, pl.ds(s, CHUNK), :]
            o = None
            for d in (0, 1):
                amat = jnp.where(masks[d], scores[2 * j + d], 0.0).astype(BF16)
                od = _dot(jnp.concatenate([q_state[2 * j + d], amat], axis=1),
                          jnp.concatenate([sb_ref[d, c], vc], axis=0))
                o = od if o is None else o + od
            outs.append(o)
        normed = [_head_norm(o, hg_ref[0]) for o in outs]
        for s, y in zip(starts, normed):
            gate = jax.nn.silu(r_ref[0, pl.ds(s, CHUNK), :].astype(F32))
            y_ref[0, pl.ds(s, CHUNK), :] = (y * gate).astype(y_ref.dtype)
        return carry

    lax.fori_loop(0, nc // GROUP, outputs, 0)


def _gla_scan(q, k, v, r, a, w2, ba, head_g):
    bsz, t, _ = q.shape
    assert t % (CHUNK * GROUP) == 0 and t % (CHUNK * GATE_GROUP) == 0, t
    nc = t // CHUNK
    qk_spec = pl.BlockSpec((1, t, DK), lambda b, h: (b, 0, h))
    v_spec = pl.BlockSpec((1, t, DV), lambda b, h: (b, 0, h))
    return pl.pallas_call(
        _gla_scan_kernel,
        grid=(bsz, HEADS),
        in_specs=[qk_spec, qk_spec, v_spec, v_spec,
                  pl.BlockSpec((1, t, 2 * GLA_RANK), lambda b, h: (b, 0, 0)),
                  pl.BlockSpec((1, 2 * GLA_RANK, 2 * DK), lambda b, h: (h, 0, 0)),
                  pl.BlockSpec((1, 1, 2 * DK), lambda b, h: (h, 0, 0)),
                  pl.BlockSpec((1, 1, DV), lambda b, h: (h, 0, 0))],
        out_specs=v_spec,
        out_shape=jax.ShapeDtypeStruct((bsz, t, HEADS * DV), BF16),
        scratch_shapes=[pltpu.VMEM((t, 2 * DK), F32),
                        pltpu.VMEM((2, nc, DK, DV), F32), pltpu.VMEM((2, nc, DK, DV), BF16)],
        compiler_params=_params(2),
        name="gla_scan",
    )(q, k, v, r, a, w2, ba, head_g)


def _mix_ffn_kernel(y_ref, x_ref, wo_ref, gn_ref, wg_ref, wu_ref, wd_ref, gl_ref, out_ref,
                    *, final):
    x1 = x_ref[...] + _dot(y_ref[...], wo_ref[...])
    h = _rmsnorm(x1, gn_ref[...]).astype(BF16)
    acc = x1
    d_ff = wg_ref.shape[1]
    for c in range(d_ff // FFN_CHUNK):
        cs = slice(c * FFN_CHUNK, (c + 1) * FFN_CHUNK)
        g = _dot(h, wg_ref[:, cs])
        u = _dot(h, wu_ref[:, cs])
        acc = acc + _dot((jax.nn.silu(g) * u).astype(BF16), wd_ref[cs, :])
    if final:
        acc = _rmsnorm(acc, gl_ref[...])
    out_ref[...] = acc


def _mix_ffn(y, x, wo, gn, wg, wu, wd, gl, final):
    n_tok, d = x.shape
    tm = TOKEN_TILE
    const = lambda i: (0, 0)
    resident = lambda w: pl.BlockSpec(w.shape, const, pipeline_mode=pl.Buffered(1))
    tile = pl.BlockSpec((tm, d), lambda i: (i, 0))
    vec = pl.BlockSpec((1, d), const)
    return pl.pallas_call(
        functools.partial(_mix_ffn_kernel, final=final),
        grid=(n_tok // tm,),
        in_specs=[tile, tile, resident(wo), vec, resident(wg), resident(wu), resident(wd), vec],
        out_specs=tile,
        out_shape=jax.ShapeDtypeStruct((n_tok, d), F32),
        compiler_params=_params(1),
        name="mix_ffn",
    )(y, x, wo, gn.reshape(1, d), wg, wu, wd, gl.reshape(1, d))


def _mlstm_mixer(x, bsz, t, g_norm, w, b_gate, head_g):
    qk = HEADS * DK
    vv = HEADS * DV
    q, k, v, o, gt = _inproj(x, g_norm, w, (DK ** -0.5, 1.0, 1.0, 1.0, 1.0),
                             (BF16, BF16, BF16, BF16, F32))
    gt = gt.reshape(bsz, t, 4, HEADS).transpose(0, 2, 3, 1)
    gi = jnp.concatenate([gt[:, 0], gt[:, 2]], axis=1)
    gf = jnp.concatenate([gt[:, 1], gt[:, 3]], axis=1)
    bg = b_gate.reshape(4, HEADS)
    bi = jnp.concatenate([bg[0], bg[2]]).reshape(2 * HEADS, 1)
    bf = jnp.concatenate([bg[1], bg[3]]).reshape(2 * HEADS, 1)
    stats = _mlstm_gates(gi, gf, bi, bf)
    rows = stats.reshape(bsz, 3, 2, HEADS, t).transpose(0, 3, 1, 2, 4).reshape(bsz, HEADS, 6, t)
    cols = rows.transpose(0, 1, 3, 2)
    return _mlstm_scan(q.reshape(bsz, t, qk), k.reshape(bsz, t, qk), v.reshape(bsz, t, vv),
                       o.reshape(bsz, t, vv), rows, cols, head_g.reshape(HEADS, 1, DV))


def _gla_mixer(x, bsz, t, g_norm, w, w2, ba, head_g):
    qk = HEADS * DK
    vv = HEADS * DV
    q, k, v, r, a = _inproj(x, g_norm, w, (DK ** -0.5, 1.0, 1.0, 1.0, 1.0),
                            (BF16, BF16, BF16, BF16, F32))
    return _gla_scan(q.reshape(bsz, t, qk), k.reshape(bsz, t, qk), v.reshape(bsz, t, vv),
                     r.reshape(bsz, t, vv), a.reshape(bsz, t, 2 * GLA_RANK), w2, ba,
                     head_g.reshape(HEADS, 1, DV))


def _split_cols(w, widths):
    out, s = [], 0
    for n in widths:
        out.append(w[:, s:s + n].astype(BF16))
        s += n
    return out


def _gla_gate_weights(w_a2, b_a):
    wf = w_a2[0].reshape(GLA_RANK, HEADS, DK).transpose(1, 0, 2)
    wb = w_a2[1].reshape(GLA_RANK, HEADS, DK).transpose(1, 0, 2)
    z = jnp.zeros_like(wf)
    w2 = jnp.concatenate([jnp.concatenate([wf, z], axis=2),
                          jnp.concatenate([z, wb], axis=2)], axis=1).astype(BF16)
    ba = jnp.concatenate([b_a[0].reshape(HEADS, 1, DK), b_a[1].reshape(HEADS, 1, DK)], axis=2)
    return w2, ba


def kernel(x_prompt, x_sample, norm_mix_g, norm_ffn_g, norm_final_g, mlstm_w_in, mlstm_b_gate, mlstm_head_g, mlstm_w_out, gla_w_in, gla_w_a2, gla_b_a, gla_head_g, gla_w_out, ffn_w_gu, ffn_w_down):
    depth = norm_mix_g.shape[0]
    d_ff = ffn_w_down.shape[1]
    qk, vv = HEADS * DK, HEADS * DV
    mlstm_w = [_split_cols(w, (qk, qk, vv, vv, 4 * HEADS)) for w in mlstm_w_in]
    gla_w = [_split_cols(w, (qk, qk, vv, vv, 2 * GLA_RANK)) for w in gla_w_in]
    gla_gate = [_gla_gate_weights(w, b) for w, b in zip(gla_w_a2, gla_b_a)]
    w_out = [mlstm_w_out.astype(BF16), gla_w_out.astype(BF16)]
    w_g = ffn_w_gu[:, :, :d_ff].astype(BF16)
    w_u = ffn_w_gu[:, :, d_ff:].astype(BF16)
    w_d = ffn_w_down.astype(BF16)

    outs = []
    for x0 in (x_prompt, x_sample):
        bsz, t, d = x0.shape
        x = x0.reshape(bsz * t, d)
        for i in range(depth):
            j = i // 2
            if i % 2 == 0:
                y = _mlstm_mixer(x, bsz, t, norm_mix_g[i], mlstm_w[j], mlstm_b_gate[j],
                                 mlstm_head_g[j])
            else:
                y = _gla_mixer(x, bsz, t, norm_mix_g[i], gla_w[j], *gla_gate[j], gla_head_g[j])
            x = _mix_ffn(y.reshape(bsz * t, vv), x, w_out[i % 2][j], norm_ffn_g[i],
                         w_g[i], w_u[i], w_d[i], norm_final_g, final=(i == depth - 1))
        outs.append(x.reshape(bsz, t, d))
    return tuple(outs)
```

```python
import functools

import jax
import jax.numpy as jnp
from jax import lax
from jax.experimental import pallas as pl
from jax.experimental.pallas import tpu as pltpu

F32 = jnp.float32
BF16 = jnp.bfloat16

EPS = 1e-6
LOG2E = 1.4426950408889634
HEADS = 4
DK = 128
DV = 256
LANES = 128
DVN = DV + LANES
GLA_RANK = 16
GLA_TAU = 16.0
NEG = -1e30

CHUNK = 128
GLA_SUB = 64
UNROLL_INDEPENDENT = 4
GROUP = 4
GATE_GROUP = 4
TOKEN_TILE = 512
FFN_CHUNK = 256
VMEM_LIMIT = 56 * 1024 * 1024


def _params(n_parallel):
    return pltpu.CompilerParams(
        dimension_semantics=("parallel",) * n_parallel,
        vmem_limit_bytes=VMEM_LIMIT)


def _rmsnorm(x, g):
    return x * lax.rsqrt(jnp.mean(x * x, axis=-1, keepdims=True) + EPS) * g


def _log_sigmoid(x):
    return jnp.minimum(x, 0.0) - jnp.log(1.0 + jnp.exp(-jnp.abs(x)))


def _dot(a, b):
    return jnp.dot(a, b, preferred_element_type=F32)


def _dot_nt(a, b):
    return lax.dot_general(a, b, (((1,), (1,)), ((), ())), preferred_element_type=F32)


def _dot_tn(a, b):
    return lax.dot_general(a, b, (((0,), (0,)), ((), ())), preferred_element_type=F32)


def _causal_masks(n):
    ti = lax.broadcasted_iota(jnp.int32, (n, n), 0)
    ji = lax.broadcasted_iota(jnp.int32, (n, n), 1)
    return ji <= ti, ji >= ti


def _inproj_kernel(x_ref, g_ref, *refs, plan):
    w_refs, o_refs = refs[:len(plan)], list(refs[len(plan):])
    h = _rmsnorm(x_ref[...], g_ref[...]).astype(BF16)
    for w_ref, (tr, parts) in zip(w_refs, plan):
        y = _dot_nt(w_ref[...], h) if tr else _dot(h, w_ref[...])
        start = 0
        for width, scale in parts:
            o_ref = o_refs.pop(0)
            part = y[start:start + width, :] if tr else y[:, start:start + width]
            if scale != 1.0:
                part = part * scale
            o_ref[...] = part.astype(o_ref.dtype)
            start += width


def _inproj(x, g, weights, plan, dtypes):
    n_tok, d = x.shape
    tm = TOKEN_TILE
    const = lambda i: (0, 0)
    out_specs, out_shape = [], []
    dts = list(dtypes)
    for tr, parts in plan:
        for width, _ in parts:
            dt = dts.pop(0)
            if tr:
                out_specs.append(pl.BlockSpec((width, tm), lambda i: (0, i)))
                out_shape.append(jax.ShapeDtypeStruct((width, n_tok), dt))
            else:
                out_specs.append(pl.BlockSpec((tm, width), lambda i: (i, 0)))
                out_shape.append(jax.ShapeDtypeStruct((n_tok, width), dt))
    return pl.pallas_call(
        functools.partial(_inproj_kernel, plan=tuple(plan)),
        grid=(n_tok // tm,),
        in_specs=[pl.BlockSpec((tm, d), lambda i: (i, 0)),
                  pl.BlockSpec((1, d), const)]
                 + [pl.BlockSpec(w.shape, const) for w in weights],
        out_specs=out_specs,
        out_shape=out_shape,
        compiler_params=_params(1),
        name="inproj",
    )(x, g.reshape(1, d), *weights)


def _scan_lanes(x, op, fill, reverse):
    t = x.shape[-1]
    lane = lax.broadcasted_iota(jnp.int32, x.shape, 1)
    s = 1
    while s < t:
        if reverse:
            x = op(x, jnp.where(lane < t - s, pltpu.roll(x, t - s, axis=1), fill))
        else:
            x = op(x, jnp.where(lane >= s, pltpu.roll(x, s, axis=1), fill))
        s *= 2
    return x


def _mlstm_gates_kernel(g_ref, bias_ref, out_ref):
    n = 2 * HEADS
    lf = _log_sigmoid(g_ref[n:, :] + bias_ref[n:, :])
    fwd = lax.broadcasted_iota(jnp.int32, lf.shape, 0) < HEADS
    b = jnp.where(fwd, _scan_lanes(lf, jnp.add, 0.0, False),
                  _scan_lanes(lf, jnp.add, 0.0, True))
    a = g_ref[:n, :] + bias_ref[:n, :] - b
    amax = jnp.where(fwd, _scan_lanes(a, jnp.maximum, NEG, False),
                     _scan_lanes(a, jnp.maximum, NEG, True))
    out_ref[0, 0] = a * LOG2E
    out_ref[0, 1] = amax * LOG2E
    out_ref[0, 2] = jnp.exp(-(b + amax))


def _mlstm_gates(g, bias, bsz, t):
    return pl.pallas_call(
        _mlstm_gates_kernel,
        grid=(bsz,),
        in_specs=[pl.BlockSpec((4 * HEADS, t), lambda b: (0, b)),
                  pl.BlockSpec((4 * HEADS, 1), lambda b: (0, 0))],
        out_specs=pl.BlockSpec((1, 3, 2 * HEADS, t), lambda b: (b, 0, 0, 0)),
        out_shape=jax.ShapeDtypeStruct((bsz, 3, 2 * HEADS, t), F32),
        compiler_params=_params(1),
        name="mlstm_gates",
    )(g, bias)


def _head_norm(hs, g):
    mean_sq = _dot((hs * hs).astype(BF16), jnp.full((DV, LANES), 1.0 / DV, BF16))
    inv = lax.rsqrt(mean_sq + EPS)
    return hs * jnp.concatenate([inv] * (DV // LANES), axis=1) * g


def _per_row(x):
    return jnp.broadcast_to(x, (LANES, x.shape[1])).T


def _mlstm_scan_kernel(q_ref, kt_ref, v_ref, o_ref, row_ref, hg_ref, y_ref,
                       dc_ref, cb_ref, ae_ref, ap_ref):
    t = q_ref.shape[1]
    nc = t // CHUNK
    masks = _causal_masks(CHUNK)
    ones = jnp.ones((CHUNK, LANES), BF16)

    def contributions(c, carry):
        s = pl.multiple_of(c * CHUNK, CHUNK)
        kt = kt_ref[:, pl.ds(s, CHUNK)].astype(F32)
        va = jnp.concatenate([v_ref[0, pl.ds(s, CHUNK), :], ones], axis=1)
        rows = row_ref[0, 0, :, pl.ds(s, CHUNK)]
        for d in (0, 1):
            edge = CHUNK - 1 if d == 0 else 0
            a_end = rows[2 + d:3 + d, edge:edge + 1]
            kw = kt * jnp.exp2(rows[d:d + 1, :] - a_end)
            dc_ref[d, c] = _dot(kw.astype(BF16), va)
            ae_ref[d, c] = jnp.broadcast_to(a_end, (8, LANES))
        return carry

    lax.fori_loop(0, nc, contributions, 0, unroll=UNROLL_INDEPENDENT)

    def recurrence(d):
        def body(i, carry):
            cst, a_prev = carry
            c = i if d == 0 else nc - 1 - i
            cb_ref[d, c] = cst.astype(BF16)
            ap_ref[d, c] = jnp.broadcast_to(a_prev, (8, LANES))
            a_end = ae_ref[d, c][0:1, 0:1]
            return jnp.exp2(a_prev - a_end) * cst + dc_ref[d, c], a_end

        lax.fori_loop(0, nc, body, (jnp.zeros((DK, DVN), F32), jnp.full((1, 1), NEG, F32)),
                      unroll=2)

    recurrence(0)
    recurrence(1)

    def outputs(i, carry):
        group = [i * GROUP + j for j in range(GROUP)]
        starts = [pl.multiple_of(c * CHUNK, CHUNK) for c in group]
        q = [q_ref[0, pl.ds(s, CHUNK), :] for s in starts]
        sm = [_dot(qc, kt_ref[:, pl.ds(s, CHUNK)]) for qc, s in zip(q, starts)]
        res, floor = [], []
        for c, s, qc, smc in zip(group, starts, q, sm):
            va = jnp.concatenate([v_ref[0, pl.ds(s, CHUNK), :], ones], axis=1)
            rows = row_ref[0, 0, :, pl.ds(s, CHUNK)]
            qf = qc.astype(F32)
            for d in (0, 1):
                a_row = rows[d:d + 1, :]
                amax = _per_row(rows[2 + d:3 + d, :])
                a_prev = ap_ref[d, c][0:1, 0:1]
                p = smc * jnp.exp2(jnp.where(masks[d], a_row - amax, NEG))
                sq = qf * jnp.exp2(a_prev - amax)
                lhs = jnp.concatenate([p.astype(BF16), sq.astype(BF16)], axis=1)
                rhs = jnp.concatenate([va, cb_ref[d, c]], axis=0)
                res.append(_dot(lhs, rhs))
                floor.append(_per_row(rows[4 + d:5 + d, :]))
        hs = []
        for j in range(GROUP):
            h = None
            for d in (0, 1):
                rd = res[2 * j + d]
                r = 1.0 / jnp.maximum(jnp.abs(rd[:, DV:]), floor[2 * j + d])
                hd = rd[:, :DV] * jnp.concatenate([r] * (DV // LANES), axis=1)
                h = hd if h is None else h + hd
            hs.append(h)
        normed = [_head_norm(h, hg_ref[0]) for h in hs]
        for s, y in zip(starts, normed):
            gate = jax.nn.sigmoid(o_ref[0, pl.ds(s, CHUNK), :].astype(F32))
            y_ref[0, pl.ds(s, CHUNK), :] = (y * gate).astype(y_ref.dtype)
        return carry

    lax.fori_loop(0, nc // GROUP, outputs, 0)


def _mlstm_scan(q, kt, v, o, rows, head_g):
    bsz, t, _ = q.shape
    assert t % (CHUNK * GROUP) == 0, t
    nc = t // CHUNK
    v_spec = pl.BlockSpec((1, t, DV), lambda b, h: (b, 0, h))
    return pl.pallas_call(
        _mlstm_scan_kernel,
        grid=(bsz, HEADS),
        in_specs=[pl.BlockSpec((1, t, DK), lambda b, h: (b, 0, h)),
                  pl.BlockSpec((DK, t), lambda b, h: (h, b)),
                  v_spec, v_spec,
                  pl.BlockSpec((1, 1, 6, t), lambda b, h: (b, h, 0, 0)),
                  pl.BlockSpec((1, 1, DV), lambda b, h: (h, 0, 0))],
        out_specs=v_spec,
        out_shape=jax.ShapeDtypeStruct((bsz, t, HEADS * DV), BF16),
        scratch_shapes=[pltpu.VMEM((2, nc, DK, DVN), F32), pltpu.VMEM((2, nc, DK, DVN), BF16),
                        pltpu.VMEM((2, nc, 8, LANES), F32), pltpu.VMEM((2, nc, 8, LANES), F32)],
        compiler_params=_params(2),
        name="mlstm_scan",
    )(q, kt, v, o, rows, head_g)


def _gla_scan_kernel(q_ref, k_ref, v_ref, r_ref, a_ref, w2_ref, ba_ref, hg_ref, y_ref,
                     bc_ref, ds_ref, sb_ref):
    t = q_ref.shape[1]
    nc = t // CHUNK
    sub = GLA_SUB
    masks = _causal_masks(CHUNK)
    tri = tuple(m.astype(BF16) for m in masks)

    def gates(i, carry):
        rows = GATE_GROUP * CHUNK
        s = pl.multiple_of(i * rows, rows)
        z = _dot(a_ref[0, pl.ds(s, rows), :].astype(BF16), w2_ref[0]) + ba_ref[0]
        lg = _log_sigmoid(z) * (LOG2E / GLA_TAU)
        hi = lg.astype(BF16)
        lo = (lg - hi.astype(F32)).astype(BF16)
        parts = []
        for j in range(GATE_GROUP):
            rs = slice(j * CHUNK, (j + 1) * CHUNK)
            for d in (0, 1):
                cs = slice(d * DK, (d + 1) * DK)
                parts.append(_dot(tri[d], jnp.concatenate([hi[rs, cs], lo[rs, cs]], axis=1)))
        for j in range(GATE_GROUP):
            for d in (0, 1):
                pd = parts[2 * j + d]
                bc_ref[pl.ds(s + j * CHUNK, CHUNK), d * DK:(d + 1) * DK] = pd[:, :DK] + pd[:, DK:]
        return carry

    lax.fori_loop(0, nc // GATE_GROUP, gates, 0)

    def contributions(c, carry):
        s = pl.multiple_of(c * CHUNK, CHUNK)
        kc = k_ref[0, pl.ds(s, CHUNK), :].astype(F32)
        vc = v_ref[0, pl.ds(s, CHUNK), :]
        for d in (0, 1):
            bc = bc_ref[pl.ds(s, CHUNK), d * DK:(d + 1) * DK]
            edge = CHUNK - 1 if d == 0 else 0
            kd = kc * jnp.exp2(bc[edge:edge + 1, :] - bc)
            ds_ref[d, c] = _dot_tn(kd.astype(BF16), vc)
        return carry

    lax.fori_loop(0, nc, contributions, 0, unroll=UNROLL_INDEPENDENT)

    eye = (lax.broadcasted_iota(jnp.int32, (DK, DK), 0)
           == lax.broadcasted_iota(jnp.int32, (DK, DK), 1))

    def recurrence(d):
        def body(i, st):
            c = i if d == 0 else nc - 1 - i
            sb_ref[d, c] = st.astype(BF16)
            tile = pl.multiple_of(c * CHUNK + (CHUNK - 8 if d == 0 else 0), 8)
            row = 7 if d == 0 else 0
            b_last = bc_ref[pl.ds(tile, 8), d * DK:(d + 1) * DK][row:row + 1, :]
            e_col = jnp.sum(jnp.where(eye, jnp.exp2(b_last), 0.0), axis=1, keepdims=True)
            return e_col * st + ds_ref[d, c]

        lax.fori_loop(0, nc, body, jnp.zeros((DK, DV), F32), unroll=UNROLL_INDEPENDENT)

    recurrence(0)
    recurrence(1)

    zeros = jnp.zeros((sub, DK), BF16)

    def outputs(i, carry):
        group = [i * GROUP + j for j in range(GROUP)]
        starts = [pl.multiple_of(c * CHUNK, CHUNK) for c in group]
        scores, q_state = [], []
        for s in starts:
            qc = q_ref[0, pl.ds(s, CHUNK), :].astype(F32)
            kc = k_ref[0, pl.ds(s, CHUNK), :].astype(F32)
            for d in (0, 1):
                bc = bc_ref[pl.ds(s, CHUNK), d * DK:(d + 1) * DK]
                first, second = ((slice(0, sub), slice(sub, CHUNK)) if d == 0
                                 else (slice(sub, CHUNK), slice(0, sub)))
                ref_row = sub - 1 if d == 0 else sub
                r = bc[ref_row:ref_row + 1, :]
                qe = (qc * jnp.exp2(bc)).astype(BF16)
                qg = (qc[second] * jnp.exp2(bc[second] - r)).astype(BF16)
                kf = (kc[first] * jnp.exp2(-bc[first])).astype(BF16)
                kg = (kc * jnp.exp2(r - bc)).astype(BF16)
                q_first = jnp.concatenate([qe[first], zeros], axis=1)
                q_second = jnp.concatenate([zeros, qg], axis=1)
                if d == 0:
                    lhs = jnp.concatenate([q_first, q_second], axis=0)
                    kf_all = jnp.concatenate([kf, zeros], axis=0)
                else:
                    lhs = jnp.concatenate([q_second, q_first], axis=0)
                    kf_all = jnp.concatenate([zeros, kf], axis=0)
                keys = jnp.concatenate([kf_all, kg], axis=1)
                scores.append(_dot_nt(lhs, keys))
                q_state.append(qe)
        outs = []
        for j, (c, s) in enumerate(zip(group, starts)):
            vc = v_ref[0, pl.ds(s, CHUNK), :]
            o = None
            for d in (0, 1):
                amat = jnp.where(masks[d], scores[2 * j + d], 0.0).astype(BF16)
                od = _dot(jnp.concatenate([q_state[2 * j + d], amat], axis=1),
                          jnp.concatenate([sb_ref[d, c], vc], axis=0))
                o = od if o is None else o + od
            outs.append(o)
        normed = [_head_norm(o, hg_ref[0]) for o in outs]
        for s, y in zip(starts, normed):
            gate = jax.nn.silu(r_ref[0, pl.ds(s, CHUNK), :].astype(F32))
            y_ref[0, pl.ds(s, CHUNK), :] = (y * gate).astype(y_ref.dtype)
        return carry

    lax.fori_loop(0, nc // GROUP, outputs, 0)


def _gla_scan(q, k, v, r, a, w2, ba, head_g):
    bsz, t, _ = q.shape
    assert t % (CHUNK * GROUP) == 0 and t % (CHUNK * GATE_GROUP) == 0, t
    nc = t // CHUNK
    qk_spec = pl.BlockSpec((1, t, DK), lambda b, h: (b, 0, h))
    v_spec = pl.BlockSpec((1, t, DV), lambda b, h: (b, 0, h))
    return pl.pallas_call(
        _gla_scan_kernel,
        grid=(bsz, HEADS),
        in_specs=[qk_spec, qk_spec, v_spec, v_spec,
                  pl.BlockSpec((1, t, 2 * GLA_RANK), lambda b, h: (b, 0, 0)),
                  pl.BlockSpec((1, 2 * GLA_RANK, 2 * DK), lambda b, h: (h, 0, 0)),
                  pl.BlockSpec((1, 1, 2 * DK), lambda b, h: (h, 0, 0)),
                  pl.BlockSpec((1, 1, DV), lambda b, h: (h, 0, 0))],
        out_specs=v_spec,
        out_shape=jax.ShapeDtypeStruct((bsz, t, HEADS * DV), BF16),
        scratch_shapes=[pltpu.VMEM((t, 2 * DK), F32),
                        pltpu.VMEM((2, nc, DK, DV), F32), pltpu.VMEM((2, nc, DK, DV), BF16)],
        compiler_params=_params(2),
        name="gla_scan",
    )(q, k, v, r, a, w2, ba, head_g)


def _mix_ffn_kernel(y_ref, x_ref, wo_ref, gn_ref, wg_ref, wu_ref, wd_ref, gl_ref, out_ref,
                    *, final):
    x1 = x_ref[...] + _dot(y_ref[...], wo_ref[...])
    h = _rmsnorm(x1, gn_ref[...]).astype(BF16)
    acc = x1
    d_ff = wg_ref.shape[1]
    for c in range(d_ff // FFN_CHUNK):
        cs = slice(c * FFN_CHUNK, (c + 1) * FFN_CHUNK)
        g = _dot(h, wg_ref[:, cs])
        u = _dot(h, wu_ref[:, cs])
        acc = acc + _dot((jax.nn.silu(g) * u).astype(BF16), wd_ref[cs, :])
    if final:
        acc = _rmsnorm(acc, gl_ref[...])
    out_ref[...] = acc


def _mix_ffn(y, x, wo, gn, wg, wu, wd, gl, final):
    n_tok, d = x.shape
    tm = TOKEN_TILE
    const = lambda i: (0, 0)
    resident = lambda w: pl.BlockSpec(w.shape, const, pipeline_mode=pl.Buffered(1))
    tile = pl.BlockSpec((tm, d), lambda i: (i, 0))
    vec = pl.BlockSpec((1, d), const)
    return pl.pallas_call(
        functools.partial(_mix_ffn_kernel, final=final),
        grid=(n_tok // tm,),
        in_specs=[tile, tile, resident(wo), vec, resident(wg), resident(wu), resident(wd), vec],
        out_specs=tile,
        out_shape=jax.ShapeDtypeStruct((n_tok, d), F32),
        compiler_params=_params(1),
        name="mix_ffn",
    )(y, x, wo, gn.reshape(1, d), wg, wu, wd, gl.reshape(1, d))


def _mlstm_mixer(x, bsz, t, g_norm, w, b_gate, head_g):
    qk = HEADS * DK
    vv = HEADS * DV
    plan = ((False, ((qk, DK ** -0.5),)), (True, ((qk, 1.0), (4 * HEADS, 1.0))),
            (False, ((vv, 1.0),)), (False, ((vv, 1.0),)))
    q, kt, gates, v, o = _inproj(x, g_norm, w, plan, (BF16, BF16, F32, BF16, BF16))
    stats = _mlstm_gates(gates, _gate_rows(b_gate).reshape(4 * HEADS, 1), bsz, t)
    rows = stats.reshape(bsz, 3, 2, HEADS, t).transpose(0, 3, 1, 2, 4).reshape(bsz, HEADS, 6, t)
    return _mlstm_scan(q.reshape(bsz, t, qk), kt, v.reshape(bsz, t, vv),
                       o.reshape(bsz, t, vv), rows, head_g.reshape(HEADS, 1, DV))


def _gla_mixer(x, bsz, t, g_norm, w, w2, ba, head_g):
    qk = HEADS * DK
    vv = HEADS * DV
    plan = tuple((False, ((n, s),)) for n, s in
                 ((qk, DK ** -0.5), (qk, 1.0), (vv, 1.0), (vv, 1.0), (2 * GLA_RANK, 1.0)))
    q, k, v, r, a = _inproj(x, g_norm, w, plan, (BF16, BF16, BF16, BF16, F32))
    return _gla_scan(q.reshape(bsz, t, qk), k.reshape(bsz, t, qk), v.reshape(bsz, t, vv),
                     r.reshape(bsz, t, vv), a.reshape(bsz, t, 2 * GLA_RANK), w2, ba,
                     head_g.reshape(HEADS, 1, DV))


def _gate_rows(g):
    g4 = g.reshape((4, HEADS) + g.shape[1:])
    return jnp.concatenate([g4[0], g4[2], g4[1], g4[3]], axis=0)


def _split_cols(w, widths):
    out, s = [], 0
    for n in widths:
        out.append(w[:, s:s + n].astype(BF16))
        s += n
    return out


def _gla_gate_weights(w_a2, b_a):
    wf = w_a2[0].reshape(GLA_RANK, HEADS, DK).transpose(1, 0, 2)
    wb = w_a2[1].reshape(GLA_RANK, HEADS, DK).transpose(1, 0, 2)
    z = jnp.zeros_like(wf)
    w2 = jnp.concatenate([jnp.concatenate([wf, z], axis=2),
                          jnp.concatenate([z, wb], axis=2)], axis=1).astype(BF16)
    ba = jnp.concatenate([b_a[0].reshape(HEADS, 1, DK), b_a[1].reshape(HEADS, 1, DK)], axis=2)
    return w2, ba


def kernel(x_prompt, x_sample, norm_mix_g, norm_ffn_g, norm_final_g, mlstm_w_in, mlstm_b_gate, mlstm_head_g, mlstm_w_out, gla_w_in, gla_w_a2, gla_b_a, gla_head_g, gla_w_out, ffn_w_gu, ffn_w_down):
    depth = norm_mix_g.shape[0]
    d_ff = ffn_w_down.shape[1]
    qk, vv = HEADS * DK, HEADS * DV
    mlstm_w = [_split_cols(w, (qk, qk, vv, vv, 4 * HEADS)) for w in mlstm_w_in]
    mlstm_w = [[wq, jnp.concatenate([wk.T, _gate_rows(wgt.T)], axis=0), wv, wo]
               for wq, wk, wv, wo, wgt in mlstm_w]
    gla_w = [_split_cols(w, (qk, qk, vv, vv, 2 * GLA_RANK)) for w in gla_w_in]
    gla_gate = [_gla_gate_weights(w, b) for w, b in zip(gla_w_a2, gla_b_a)]
    w_out = [mlstm_w_out.astype(BF16), gla_w_out.astype(BF16)]
    w_g = ffn_w_gu[:, :, :d_ff].astype(BF16)
    w_u = ffn_w_gu[:, :, d_ff:].astype(BF16)
    w_d = ffn_w_down.astype(BF16)

    outs = []
    for x0 in (x_prompt, x_sample):
        bsz, t, d = x0.shape
        x = x0.reshape(bsz * t, d)
        for i in range(depth):
            j = i // 2
            if i % 2 == 0:
                y = _mlstm_mixer(x, bsz, t, norm_mix_g[i], mlstm_w[j], mlstm_b_gate[j],
                                 mlstm_head_g[j])
            else:
                y = _gla_mixer(x, bsz, t, norm_mix_g[i], gla_w[j], *gla_gate[j], gla_head_g[j])
            x = _mix_ffn(y.reshape(bsz * t, vv), x, w_out[i % 2][j], norm_ffn_g[i],
                         w_g[i], w_u[i], w_d[i], norm_final_g, final=(i == depth - 1))
        outs.append(x.reshape(bsz, t, d))
    return tuple(outs)
```

```python
import functools

import jax
import jax.numpy as jnp
from jax import lax
from jax.experimental import pallas as pl
from jax.experimental.pallas import tpu as pltpu

F32 = jnp.float32
BF16 = jnp.bfloat16

EPS = 1e-6
LOG2E = 1.4426950408889634
HEADS = 4
DK = 128
DV = 256
LANES = 128
DVN = DV + LANES
GLA_RANK = 16
GLA_TAU = 16.0
NEG = -1e30

CHUNK = 128
GLA_SUB = 64
STATE_UNROLL = 8
GROUP = 4
GATE_GROUP = 4
TOKEN_TILE = 512
FFN_CHUNK = 256
VMEM_LIMIT = 56 * 1024 * 1024


def _params(n_parallel):
    return pltpu.CompilerParams(
        dimension_semantics=("parallel",) * n_parallel,
        vmem_limit_bytes=VMEM_LIMIT)


def _rmsnorm(x, g):
    return x * lax.rsqrt(jnp.mean(x * x, axis=-1, keepdims=True) + EPS) * g


def _log_sigmoid(x):
    return jnp.minimum(x, 0.0) - jnp.log(1.0 + jnp.exp(-jnp.abs(x)))


def _dot(a, b):
    return jnp.dot(a, b, preferred_element_type=F32)


def _dot_nt(a, b):
    return lax.dot_general(a, b, (((1,), (1,)), ((), ())), preferred_element_type=F32)


def _dot_tn(a, b):
    return lax.dot_general(a, b, (((0,), (0,)), ((), ())), preferred_element_type=F32)


def _chunk_start(c):
    return c * CHUNK if isinstance(c, int) else pl.multiple_of(c * CHUNK, CHUNK)


def _causal_masks(n):
    ti = lax.broadcasted_iota(jnp.int32, (n, n), 0)
    ji = lax.broadcasted_iota(jnp.int32, (n, n), 1)
    return ji <= ti, ji >= ti


def _inproj_kernel(x_ref, g_ref, *refs, plan):
    w_refs, o_refs = refs[:len(plan)], list(refs[len(plan):])
    h = _rmsnorm(x_ref[...], g_ref[...]).astype(BF16)
    for w_ref, (tr, parts) in zip(w_refs, plan):
        y = _dot_nt(w_ref[...], h) if tr else _dot(h, w_ref[...])
        start = 0
        for width, scale in parts:
            o_ref = o_refs.pop(0)
            part = y[start:start + width, :] if tr else y[:, start:start + width]
            if scale != 1.0:
                part = part * scale
            o_ref[...] = part.astype(o_ref.dtype)
            start += width


def _inproj(x, g, weights, plan, dtypes):
    n_tok, d = x.shape
    tm = TOKEN_TILE
    const = lambda i: (0, 0)
    out_specs, out_shape = [], []
    dts = list(dtypes)
    for tr, parts in plan:
        for width, _ in parts:
            dt = dts.pop(0)
            if tr:
                out_specs.append(pl.BlockSpec((width, tm), lambda i: (0, i)))
                out_shape.append(jax.ShapeDtypeStruct((width, n_tok), dt))
            else:
                out_specs.append(pl.BlockSpec((tm, width), lambda i: (i, 0)))
                out_shape.append(jax.ShapeDtypeStruct((n_tok, width), dt))
    return pl.pallas_call(
        functools.partial(_inproj_kernel, plan=tuple(plan)),
        grid=(n_tok // tm,),
        in_specs=[pl.BlockSpec((tm, d), lambda i: (i, 0)),
                  pl.BlockSpec((1, d), const)]
                 + [pl.BlockSpec(w.shape, const) for w in weights],
        out_specs=out_specs,
        out_shape=out_shape,
        compiler_params=_params(1),
        name="inproj",
    )(x, g.reshape(1, d), *weights)


def _scan_lanes(x, op, fill, reverse):
    t = x.shape[-1]
    lane = lax.broadcasted_iota(jnp.int32, x.shape, 1)
    s = 1
    while s < t:
        if reverse:
            x = op(x, jnp.where(lane < t - s, pltpu.roll(x, t - s, axis=1), fill))
        else:
            x = op(x, jnp.where(lane >= s, pltpu.roll(x, s, axis=1), fill))
        s *= 2
    return x


def _mlstm_gates_kernel(g_ref, bias_ref, out_ref):
    n = 2 * HEADS
    lf = _log_sigmoid(g_ref[n:, :] + bias_ref[n:, :])
    fwd = lax.broadcasted_iota(jnp.int32, lf.shape, 0) < HEADS
    b = jnp.where(fwd, _scan_lanes(lf, jnp.add, 0.0, False),
                  _scan_lanes(lf, jnp.add, 0.0, True))
    a = g_ref[:n, :] + bias_ref[:n, :] - b
    amax = jnp.where(fwd, _scan_lanes(a, jnp.maximum, NEG, False),
                     _scan_lanes(a, jnp.maximum, NEG, True))
    out_ref[0, 0] = a * LOG2E
    out_ref[0, 1] = amax * LOG2E
    out_ref[0, 2] = jnp.exp(-(b + amax))


def _mlstm_gates(g, bias, bsz, t):
    return pl.pallas_call(
        _mlstm_gates_kernel,
        grid=(bsz,),
        in_specs=[pl.BlockSpec((4 * HEADS, t), lambda b: (0, b)),
                  pl.BlockSpec((4 * HEADS, 1), lambda b: (0, 0))],
        out_specs=pl.BlockSpec((1, 3, 2 * HEADS, t), lambda b: (b, 0, 0, 0)),
        out_shape=jax.ShapeDtypeStruct((bsz, 3, 2 * HEADS, t), F32),
        compiler_params=_params(1),
        name="mlstm_gates",
    )(g, bias)


def _head_norm(hs, g):
    mean_sq = _dot((hs * hs).astype(BF16), jnp.full((DV, LANES), 1.0 / DV, BF16))
    inv = lax.rsqrt(mean_sq + EPS)
    return hs * jnp.concatenate([inv] * (DV // LANES), axis=1) * g


def _per_row(x):
    return jnp.broadcast_to(x, (LANES, x.shape[1])).T


def _mlstm_scan_kernel(q_ref, kt_ref, v_ref, o_ref, row_ref, hg_ref, y_ref,
                       st_ref, cb_ref, ap_ref):
    t = q_ref.shape[1]
    nc = t // CHUNK
    masks = _causal_masks(CHUNK)
    ones = jnp.ones((CHUNK, LANES), BF16)

    st_ref[...] = jnp.zeros_like(st_ref)

    def states(i, a_prevs):
        a_ends = []
        for d, a_prev in enumerate(a_prevs):
            c = i if d == 0 else nc - 1 - i
            s = _chunk_start(c)
            kt = kt_ref[:, pl.ds(s, CHUNK)].astype(F32)
            va = jnp.concatenate([v_ref[0, pl.ds(s, CHUNK), :], ones], axis=1)
            rows = row_ref[0, 0, :, pl.ds(s, CHUNK)]
            edge = CHUNK - 1 if d == 0 else 0
            a_end = rows[2 + d:3 + d, edge:edge + 1]
            kw = kt * jnp.exp2(rows[d:d + 1, :] - a_end)
            state = st_ref[d]
            cb_ref[d, c] = state.astype(BF16)
            ap_ref[d, c] = jnp.broadcast_to(a_prev, (8, LANES))
            st_ref[d] = jnp.exp2(a_prev - a_end) * state + _dot(kw.astype(BF16), va)
            a_ends.append(a_end)
        return tuple(a_ends)

    lax.fori_loop(0, nc, states, (jnp.full((1, 1), NEG, F32),) * 2, unroll=STATE_UNROLL)

    def scores_of(g):
        starts = [_chunk_start(g * GROUP + j) for j in range(GROUP)]
        return [_dot(q_ref[0, pl.ds(s, CHUNK), :], kt_ref[:, pl.ds(s, CHUNK)]) for s in starts]

    def weighted(g, scores):
        lhs = []
        for j, sm in enumerate(scores):
            c = g * GROUP + j
            s = _chunk_start(c)
            rows = row_ref[0, 0, :, pl.ds(s, CHUNK)]
            qf = q_ref[0, pl.ds(s, CHUNK), :].astype(F32)
            for d in (0, 1):
                a_row = rows[d:d + 1, :]
                amax = _per_row(rows[2 + d:3 + d, :])
                a_prev = ap_ref[d, c][0:1, 0:1]
                p = sm * jnp.exp2(jnp.where(masks[d], a_row - amax, NEG))
                sq = qf * jnp.exp2(a_prev - amax)
                lhs.append(jnp.concatenate([p.astype(BF16), sq.astype(BF16)], axis=1))
        return tuple(lhs)

    def readout(g, lhs):
        res = []
        for j in range(GROUP):
            c = g * GROUP + j
            va = jnp.concatenate([v_ref[0, pl.ds(_chunk_start(c), CHUNK), :], ones], axis=1)
            for d in (0, 1):
                res.append(_dot(lhs[2 * j + d], jnp.concatenate([va, cb_ref[d, c]], axis=0)))
        return res

    def finish(g, res):
        hs = []
        for j in range(GROUP):
            rows = row_ref[0, 0, :, pl.ds(_chunk_start(g * GROUP + j), CHUNK)]
            h = None
            for d in (0, 1):
                rd = res[2 * j + d]
                r = 1.0 / jnp.maximum(jnp.abs(rd[:, DV:]), _per_row(rows[4 + d:5 + d, :]))
                hd = rd[:, :DV] * jnp.concatenate([r] * (DV // LANES), axis=1)
                h = hd if h is None else h + hd
            hs.append(h)
        normed = [_head_norm(h, hg_ref[0]) for h in hs]
        for j, y in enumerate(normed):
            s = _chunk_start(g * GROUP + j)
            gate = jax.nn.sigmoid(o_ref[0, pl.ds(s, CHUNK), :].astype(F32))
            y_ref[0, pl.ds(s, CHUNK), :] = (y * gate).astype(y_ref.dtype)

    def step(g, carry):
        finish(g, readout(g, weighted(g, scores_of(g))))
        return carry

    lax.fori_loop(0, nc // GROUP, step, 0)


def _mlstm_scan(q, kt, v, o, rows, head_g):
    bsz, t, _ = q.shape
    assert t % (CHUNK * GROUP) == 0, t
    nc = t // CHUNK
    v_spec = pl.BlockSpec((1, t, DV), lambda b, h: (b, 0, h))
    return pl.pallas_call(
        _mlstm_scan_kernel,
        grid=(bsz, HEADS),
        in_specs=[pl.BlockSpec((1, t, DK), lambda b, h: (b, 0, h)),
                  pl.BlockSpec((DK, t), lambda b, h: (h, b)),
                  v_spec, v_spec,
                  pl.BlockSpec((1, 1, 6, t), lambda b, h: (b, h, 0, 0)),
                  pl.BlockSpec((1, 1, DV), lambda b, h: (h, 0, 0))],
        out_specs=v_spec,
        out_shape=jax.ShapeDtypeStruct((bsz, t, HEADS * DV), BF16),
        scratch_shapes=[pltpu.VMEM((2, DK, DVN), F32), pltpu.VMEM((2, nc, DK, DVN), BF16),
                        pltpu.VMEM((2, nc, 8, LANES), F32)],
        compiler_params=_params(2),
        name="mlstm_scan",
    )(q, kt, v, o, rows, head_g)


def _gla_scan_kernel(q_ref, k_ref, v_ref, r_ref, a_ref, w2_ref, ba_ref, hg_ref, y_ref,
                     bc_ref, st_ref, sb_ref):
    t = q_ref.shape[1]
    nc = t // CHUNK
    sub = GLA_SUB
    masks = _causal_masks(CHUNK)
    tri = tuple(m.astype(BF16) for m in masks)

    def gates(i, carry):
        rows = GATE_GROUP * CHUNK
        s = pl.multiple_of(i * rows, rows)
        z = _dot(a_ref[0, pl.ds(s, rows), :].astype(BF16), w2_ref[0]) + ba_ref[0]
        lg = _log_sigmoid(z) * (LOG2E / GLA_TAU)
        hi = lg.astype(BF16)
        lo = (lg - hi.astype(F32)).astype(BF16)
        parts = []
        for j in range(GATE_GROUP):
            rs = slice(j * CHUNK, (j + 1) * CHUNK)
            for d in (0, 1):
                cs = slice(d * DK, (d + 1) * DK)
                parts.append(_dot(tri[d], jnp.concatenate([hi[rs, cs], lo[rs, cs]], axis=1)))
        for j in range(GATE_GROUP):
            for d in (0, 1):
                pd = parts[2 * j + d]
                bc_ref[pl.ds(s + j * CHUNK, CHUNK), d * DK:(d + 1) * DK] = pd[:, :DK] + pd[:, DK:]
        return carry

    lax.fori_loop(0, nc // GATE_GROUP, gates, 0)

    eye = (lax.broadcasted_iota(jnp.int32, (DK, DK), 0)
           == lax.broadcasted_iota(jnp.int32, (DK, DK), 1))

    st_ref[...] = jnp.zeros_like(st_ref)

    def states(i, carry):
        for d in (0, 1):
            c = i if d == 0 else nc - 1 - i
            s = _chunk_start(c)
            kc = k_ref[0, pl.ds(s, CHUNK), :].astype(F32)
            bc = bc_ref[pl.ds(s, CHUNK), d * DK:(d + 1) * DK]
            edge = CHUNK - 1 if d == 0 else 0
            b_last = bc[edge:edge + 1, :]
            kd = kc * jnp.exp2(b_last - bc)
            e_col = jnp.sum(jnp.where(eye, jnp.exp2(b_last), 0.0), axis=1, keepdims=True)
            state = st_ref[d]
            sb_ref[d, c] = state.astype(BF16)
            st_ref[d] = e_col * state + _dot_tn(kd.astype(BF16), v_ref[0, pl.ds(s, CHUNK), :])
        return carry

    lax.fori_loop(0, nc, states, 0, unroll=STATE_UNROLL)

    zeros = jnp.zeros((sub, DK), BF16)

    def scores_of(g):
        starts = [_chunk_start(g * GROUP + j) for j in range(GROUP)]
        scores, q_state = [], []
        for s in starts:
            qc = q_ref[0, pl.ds(s, CHUNK), :].astype(F32)
            kc = k_ref[0, pl.ds(s, CHUNK), :].astype(F32)
            for d in (0, 1):
                bc = bc_ref[pl.ds(s, CHUNK), d * DK:(d + 1) * DK]
                first, second = ((slice(0, sub), slice(sub, CHUNK)) if d == 0
                                 else (slice(sub, CHUNK), slice(0, sub)))
                ref_row = sub - 1 if d == 0 else sub
                r = bc[ref_row:ref_row + 1, :]
                qe = (qc * jnp.exp2(bc)).astype(BF16)
                qg = (qc[second] * jnp.exp2(bc[second] - r)).astype(BF16)
                kf = (kc[first] * jnp.exp2(-bc[first])).astype(BF16)
                kg = (kc * jnp.exp2(r - bc)).astype(BF16)
                q_first = jnp.concatenate([qe[first], zeros], axis=1)
                q_second = jnp.concatenate([zeros, qg], axis=1)
                if d == 0:
                    lhs = jnp.concatenate([q_first, q_second], axis=0)
                    kf_all = jnp.concatenate([kf, zeros], axis=0)
                else:
                    lhs = jnp.concatenate([q_second, q_first], axis=0)
                    kf_all = jnp.concatenate([zeros, kf], axis=0)
                keys = jnp.concatenate([kf_all, kg], axis=1)
                scores.append(_dot_nt(lhs, keys))
                q_state.append(qe)
        return scores, tuple(q_state)

    def masked(scores):
        return tuple(jnp.where(masks[n % 2], sc, 0.0).astype(BF16) for n, sc in enumerate(scores))

    def readout(g, amats, q_state):
        outs = []
        for j in range(GROUP):
            c = g * GROUP + j
            vc = v_ref[0, pl.ds(_chunk_start(c), CHUNK), :]
            o = None
            for d in (0, 1):
                od = _dot(jnp.concatenate([q_state[2 * j + d], amats[2 * j + d]], axis=1),
                          jnp.concatenate([sb_ref[d, c], vc], axis=0))
                o = od if o is None else o + od
            outs.append(o)
        return outs

    def finish(g, outs):
        normed = [_head_norm(o, hg_ref[0]) for o in outs]
        for j, y in enumerate(normed):
            s = _chunk_start(g * GROUP + j)
            gate = jax.nn.silu(r_ref[0, pl.ds(s, CHUNK), :].astype(F32))
            y_ref[0, pl.ds(s, CHUNK), :] = (y * gate).astype(y_ref.dtype)

    def step(g, carry):
        scores, q_state = scores_of(g)
        finish(g, readout(g, masked(scores), q_state))
        return carry

    lax.fori_loop(0, nc // GROUP, step, 0)


def _gla_scan(q, k, v, r, a, w2, ba, head_g):
    bsz, t, _ = q.shape
    assert t % (CHUNK * GROUP) == 0 and t % (CHUNK * GATE_GROUP) == 0, t
    nc = t // CHUNK
    qk_spec = pl.BlockSpec((1, t, DK), lambda b, h: (b, 0, h))
    v_spec = pl.BlockSpec((1, t, DV), lambda b, h: (b, 0, h))
    return pl.pallas_call(
        _gla_scan_kernel,
        grid=(bsz, HEADS),
        in_specs=[qk_spec, qk_spec, v_spec, v_spec,
                  pl.BlockSpec((1, t, 2 * GLA_RANK), lambda b, h: (b, 0, 0)),
                  pl.BlockSpec((1, 2 * GLA_RANK, 2 * DK), lambda b, h: (h, 0, 0)),
                  pl.BlockSpec((1, 1, 2 * DK), lambda b, h: (h, 0, 0)),
                  pl.BlockSpec((1, 1, DV), lambda b, h: (h, 0, 0))],
        out_specs=v_spec,
        out_shape=jax.ShapeDtypeStruct((bsz, t, HEADS * DV), BF16),
        scratch_shapes=[pltpu.VMEM((t, 2 * DK), F32),
                        pltpu.VMEM((2, DK, DV), F32), pltpu.VMEM((2, nc, DK, DV), BF16)],
        compiler_params=_params(2),
        name="gla_scan",
    )(q, k, v, r, a, w2, ba, head_g)


def _mix_ffn_kernel(y_ref, x_ref, wo_ref, gn_ref, wg_ref, wu_ref, wd_ref, gl_ref, out_ref,
                    *, final):
    x1 = x_ref[...] + _dot(y_ref[...], wo_ref[...])
    h = _rmsnorm(x1, gn_ref[...]).astype(BF16)
    acc = x1
    d_ff = wg_ref.shape[1]
    for c in range(d_ff // FFN_CHUNK):
        cs = slice(c * FFN_CHUNK, (c + 1) * FFN_CHUNK)
        g = _dot(h, wg_ref[:, cs])
        u = _dot(h, wu_ref[:, cs])
        acc = acc + _dot((jax.nn.silu(g) * u).astype(BF16), wd_ref[cs, :])
    if final:
        acc = _rmsnorm(acc, gl_ref[...])
    out_ref[...] = acc


def _mix_ffn(y, x, wo, gn, wg, wu, wd, gl, final):
    n_tok, d = x.shape
    tm = TOKEN_TILE
    const = lambda i: (0, 0)
    resident = lambda w: pl.BlockSpec(w.shape, const, pipeline_mode=pl.Buffered(1))
    tile = pl.BlockSpec((tm, d), lambda i: (i, 0))
    vec = pl.BlockSpec((1, d), const)
    return pl.pallas_call(
        functools.partial(_mix_ffn_kernel, final=final),
        grid=(n_tok // tm,),
        in_specs=[tile, tile, resident(wo), vec, resident(wg), resident(wu), resident(wd), vec],
        out_specs=tile,
        out_shape=jax.ShapeDtypeStruct((n_tok, d), F32),
        compiler_params=_params(1),
        name="mix_ffn",
    )(y, x, wo, gn.reshape(1, d), wg, wu, wd, gl.reshape(1, d))


def _mlstm_mixer(x, bsz, t, g_norm, w, b_gate, head_g):
    qk = HEADS * DK
    vv = HEADS * DV
    plan = ((False, ((qk, DK ** -0.5),)), (True, ((qk, 1.0), (4 * HEADS, 1.0))),
            (False, ((vv, 1.0),)), (False, ((vv, 1.0),)))
    q, kt, gates, v, o = _inproj(x, g_norm, w, plan, (BF16, BF16, F32, BF16, BF16))
    stats = _mlstm_gates(gates, _gate_rows(b_gate).reshape(4 * HEADS, 1), bsz, t)
    rows = stats.reshape(bsz, 3, 2, HEADS, t).transpose(0, 3, 1, 2, 4).reshape(bsz, HEADS, 6, t)
    return _mlstm_scan(q.reshape(bsz, t, qk), kt, v.reshape(bsz, t, vv),
                       o.reshape(bsz, t, vv), rows, head_g.reshape(HEADS, 1, DV))


def _gla_mixer(x, bsz, t, g_norm, w, w2, ba, head_g):
    qk = HEADS * DK
    vv = HEADS * DV
    plan = tuple((False, ((n, s),)) for n, s in
                 ((qk, DK ** -0.5), (qk, 1.0), (vv, 1.0), (vv, 1.0), (2 * GLA_RANK, 1.0)))
    q, k, v, r, a = _inproj(x, g_norm, w, plan, (BF16, BF16, BF16, BF16, F32))
    return _gla_scan(q.reshape(bsz, t, qk), k.reshape(bsz, t, qk), v.reshape(bsz, t, vv),
                     r.reshape(bsz, t, vv), a.reshape(bsz, t, 2 * GLA_RANK), w2, ba,
                     head_g.reshape(HEADS, 1, DV))


def _gate_rows(g):
    g4 = g.reshape((4, HEADS) + g.shape[1:])
    return jnp.concatenate([g4[0], g4[2], g4[1], g4[3]], axis=0)


def _split_cols(w, widths):
    out, s = [], 0
    for n in widths:
        out.append(w[:, s:s + n].astype(BF16))
        s += n
    return out


def _gla_gate_weights(w_a2, b_a):
    wf = w_a2[0].reshape(GLA_RANK, HEADS, DK).transpose(1, 0, 2)
    wb = w_a2[1].reshape(GLA_RANK, HEADS, DK).transpose(1, 0, 2)
    z = jnp.zeros_like(wf)
    w2 = jnp.concatenate([jnp.concatenate([wf, z], axis=2),
                          jnp.concatenate([z, wb], axis=2)], axis=1).astype(BF16)
    ba = jnp.concatenate([b_a[0].reshape(HEADS, 1, DK), b_a[1].reshape(HEADS, 1, DK)], axis=2)
    return w2, ba


def kernel(x_prompt, x_sample, norm_mix_g, norm_ffn_g, norm_final_g, mlstm_w_in, mlstm_b_gate, mlstm_head_g, mlstm_w_out, gla_w_in, gla_w_a2, gla_b_a, gla_head_g, gla_w_out, ffn_w_gu, ffn_w_down):
    depth = norm_mix_g.shape[0]
    d_ff = ffn_w_down.shape[1]
    qk, vv = HEADS * DK, HEADS * DV
    mlstm_w = [_split_cols(w, (qk, qk, vv, vv, 4 * HEADS)) for w in mlstm_w_in]
    mlstm_w = [[wq, jnp.concatenate([wk.T, _gate_rows(wgt.T)], axis=0), wv, wo]
               for wq, wk, wv, wo, wgt in mlstm_w]
    gla_w = [_split_cols(w, (qk, qk, vv, vv, 2 * GLA_RANK)) for w in gla_w_in]
    gla_gate = [_gla_gate_weights(w, b) for w, b in zip(gla_w_a2, gla_b_a)]
    w_out = [mlstm_w_out.astype(BF16), gla_w_out.astype(BF16)]
    w_g = ffn_w_gu[:, :, :d_ff].astype(BF16)
    w_u = ffn_w_gu[:, :, d_ff:].astype(BF16)
    w_d = ffn_w_down.astype(BF16)

    outs = []
    for x0 in (x_prompt, x_sample):
        bsz, t, d = x0.shape
        x = x0.reshape(bsz * t, d)
        for i in range(depth):
            j = i // 2
            if i % 2 == 0:
                y = _mlstm_mixer(x, bsz, t, norm_mix_g[i], mlstm_w[j], mlstm_b_gate[j],
                                 mlstm_head_g[j])
            else:
                y = _gla_mixer(x, bsz, t, norm_mix_g[i], gla_w[j], *gla_gate[j], gla_head_g[j])
            x = _mix_ffn(y.reshape(bsz * t, vv), x, w_out[i % 2][j], norm_ffn_g[i],
                         w_g[i], w_u[i], w_d[i], norm_final_g, final=(i == depth - 1))
        outs.append(x.reshape(bsz, t, d))
    return tuple(outs)
```

```python
import functools

import jax
import jax.numpy as jnp
from jax import lax
from jax.experimental import pallas as pl
from jax.experimental.pallas import tpu as pltpu

F32 = jnp.float32
BF16 = jnp.bfloat16

EPS = 1e-6
LOG2E = 1.4426950408889634
HEADS = 4
DK = 128
DV = 256
LANES = 128
DVN = DV + LANES
GLA_RANK = 16
GLA_TAU = 16.0
NEG = -1e30

CHUNK = 128
GLA_SUB = 64
STATE_UNROLL = 8
GROUP = 4
GATE_GROUP = 4
TOKEN_TILE = 512
FFN_CHUNK = 256
VMEM_LIMIT = 56 * 1024 * 1024


def _params(n_parallel):
    return pltpu.CompilerParams(
        dimension_semantics=("parallel",) * n_parallel,
        vmem_limit_bytes=VMEM_LIMIT)


def _rmsnorm(x, g):
    return x * lax.rsqrt(jnp.mean(x * x, axis=-1, keepdims=True) + EPS) * g


def _log_sigmoid(x):
    return jnp.minimum(x, 0.0) - jnp.log(1.0 + jnp.exp(-jnp.abs(x)))


def _dot(a, b):
    return jnp.dot(a, b, preferred_element_type=F32)


def _dot_nt(a, b):
    return lax.dot_general(a, b, (((1,), (1,)), ((), ())), preferred_element_type=F32)


def _dot_tn(a, b):
    return lax.dot_general(a, b, (((0,), (0,)), ((), ())), preferred_element_type=F32)


def _chunk_start(c):
    return c * CHUNK if isinstance(c, int) else pl.multiple_of(c * CHUNK, CHUNK)


def _causal_masks(n):
    ti = lax.broadcasted_iota(jnp.int32, (n, n), 0)
    ji = lax.broadcasted_iota(jnp.int32, (n, n), 1)
    return ji <= ti, ji >= ti


def _inproj_kernel(x_ref, g_ref, *refs, plan):
    w_refs, o_refs = refs[:len(plan)], list(refs[len(plan):])
    h = _rmsnorm(x_ref[...], g_ref[...]).astype(BF16)
    for w_ref, (tr, parts) in zip(w_refs, plan):
        y = _dot_nt(w_ref[...], h) if tr else _dot(h, w_ref[...])
        start = 0
        for width, scale, head_dim in parts:
            o_ref = o_refs.pop(0)
            part = y[start:start + width, :] if tr else y[:, start:start + width]
            if scale != 1.0:
                part = part * scale
            part = part.astype(o_ref.dtype)
            if head_dim is None:
                o_ref[...] = part
            else:
                for hd in range(width // head_dim):
                    o_ref[hd] = part[:, hd * head_dim:(hd + 1) * head_dim]
            start += width


def _inproj(x, g, weights, plan, dtypes):
    n_tok, d = x.shape
    tm = TOKEN_TILE
    const = lambda i: (0, 0)
    out_specs, out_shape = [], []
    dts = list(dtypes)
    for tr, parts in plan:
        for width, _, head_dim in parts:
            dt = dts.pop(0)
            if tr:
                out_specs.append(pl.BlockSpec((width, tm), lambda i: (0, i)))
                out_shape.append(jax.ShapeDtypeStruct((width, n_tok), dt))
            elif head_dim is None:
                out_specs.append(pl.BlockSpec((tm, width), lambda i: (i, 0)))
                out_shape.append(jax.ShapeDtypeStruct((n_tok, width), dt))
            else:
                nh = width // head_dim
                out_specs.append(pl.BlockSpec((nh, tm, head_dim), lambda i: (0, i, 0)))
                out_shape.append(jax.ShapeDtypeStruct((nh, n_tok, head_dim), dt))
    return pl.pallas_call(
        functools.partial(_inproj_kernel, plan=tuple(plan)),
        grid=(n_tok // tm,),
        in_specs=[pl.BlockSpec((tm, d), lambda i: (i, 0)),
                  pl.BlockSpec((1, d), const)]
                 + [pl.BlockSpec(w.shape, const) for w in weights],
        out_specs=out_specs,
        out_shape=out_shape,
        compiler_params=_params(1),
        name="inproj",
    )(x, g.reshape(1, d), *weights)


def _scan_lanes(x, op, fill, reverse):
    t = x.shape[-1]
    lane = lax.broadcasted_iota(jnp.int32, x.shape, 1)
    s = 1
    while s < t:
        if reverse:
            x = op(x, jnp.where(lane < t - s, pltpu.roll(x, t - s, axis=1), fill))
        else:
            x = op(x, jnp.where(lane >= s, pltpu.roll(x, s, axis=1), fill))
        s *= 2
    return x


def _mlstm_gates_kernel(g_ref, bias_ref, out_ref):
    n = 2 * HEADS
    lf = _log_sigmoid(g_ref[n:, :] + bias_ref[n:, :])
    fwd = lax.broadcasted_iota(jnp.int32, lf.shape, 0) < HEADS
    b = jnp.where(fwd, _scan_lanes(lf, jnp.add, 0.0, False),
                  _scan_lanes(lf, jnp.add, 0.0, True))
    a = g_ref[:n, :] + bias_ref[:n, :] - b
    amax = jnp.where(fwd, _scan_lanes(a, jnp.maximum, NEG, False),
                     _scan_lanes(a, jnp.maximum, NEG, True))
    out_ref[0, 0] = a * LOG2E
    out_ref[0, 1] = amax * LOG2E
    out_ref[0, 2] = jnp.exp(-(b + amax))


def _mlstm_gates(g, bias, bsz, t):
    return pl.pallas_call(
        _mlstm_gates_kernel,
        grid=(bsz,),
        in_specs=[pl.BlockSpec((4 * HEADS, t), lambda b: (0, b)),
                  pl.BlockSpec((4 * HEADS, 1), lambda b: (0, 0))],
        out_specs=pl.BlockSpec((1, 3, 2 * HEADS, t), lambda b: (b, 0, 0, 0)),
        out_shape=jax.ShapeDtypeStruct((bsz, 3, 2 * HEADS, t), F32),
        compiler_params=_params(1),
        name="mlstm_gates",
    )(g, bias)


def _head_norm(hs, g):
    mean_sq = _dot((hs * hs).astype(BF16), jnp.full((DV, LANES), 1.0 / DV, BF16))
    inv = lax.rsqrt(mean_sq + EPS)
    return hs * jnp.concatenate([inv] * (DV // LANES), axis=1) * g


def _per_row(x):
    return jnp.broadcast_to(x, (LANES, x.shape[1])).T


def _mlstm_scan_kernel(q_ref, kt_ref, v_ref, o_ref, row_ref, hg_ref, y_ref,
                       st_ref, cb_ref, ap_ref):
    t = q_ref.shape[1]
    nc = t // CHUNK
    masks = _causal_masks(CHUNK)
    ones = jnp.ones((CHUNK, LANES), BF16)

    st_ref[...] = jnp.zeros_like(st_ref)

    def states(i, a_prevs):
        a_ends = []
        for d, a_prev in enumerate(a_prevs):
            c = i if d == 0 else nc - 1 - i
            s = _chunk_start(c)
            kt = kt_ref[:, pl.ds(s, CHUNK)].astype(F32)
            va = jnp.concatenate([v_ref[0, pl.ds(s, CHUNK), :], ones], axis=1)
            rows = row_ref[0, 0, :, pl.ds(s, CHUNK)]
            edge = CHUNK - 1 if d == 0 else 0
            a_end = rows[2 + d:3 + d, edge:edge + 1]
            kw = kt * jnp.exp2(rows[d:d + 1, :] - a_end)
            state = st_ref[d]
            cb_ref[d, c] = state.astype(BF16)
            ap_ref[d, c] = jnp.broadcast_to(a_prev, (8, LANES))
            st_ref[d] = jnp.exp2(a_prev - a_end) * state + _dot(kw.astype(BF16), va)
            a_ends.append(a_end)
        return tuple(a_ends)

    lax.fori_loop(0, nc, states, (jnp.full((1, 1), NEG, F32),) * 2, unroll=STATE_UNROLL)

    def scores_of(g):
        starts = [_chunk_start(g * GROUP + j) for j in range(GROUP)]
        return [_dot(q_ref[0, pl.ds(s, CHUNK), :], kt_ref[:, pl.ds(s, CHUNK)]) for s in starts]

    def weighted(g, scores):
        lhs = []
        for j, sm in enumerate(scores):
            c = g * GROUP + j
            s = _chunk_start(c)
            rows = row_ref[0, 0, :, pl.ds(s, CHUNK)]
            qf = q_ref[0, pl.ds(s, CHUNK), :].astype(F32)
            for d in (0, 1):
                a_row = rows[d:d + 1, :]
                amax = _per_row(rows[2 + d:3 + d, :])
                a_prev = ap_ref[d, c][0:1, 0:1]
                p = sm * jnp.exp2(jnp.where(masks[d], a_row - amax, NEG))
                sq = qf * jnp.exp2(a_prev - amax)
                lhs.append(jnp.concatenate([p.astype(BF16), sq.astype(BF16)], axis=1))
        return tuple(lhs)

    def readout(g, lhs):
        res = []
        for j in range(GROUP):
            c = g * GROUP + j
            va = jnp.concatenate([v_ref[0, pl.ds(_chunk_start(c), CHUNK), :], ones], axis=1)
            for d in (0, 1):
                res.append(_dot(lhs[2 * j + d], jnp.concatenate([va, cb_ref[d, c]], axis=0)))
        return res

    def finish(g, res):
        hs = []
        for j in range(GROUP):
            rows = row_ref[0, 0, :, pl.ds(_chunk_start(g * GROUP + j), CHUNK)]
            h = None
            for d in (0, 1):
                rd = res[2 * j + d]
                r = 1.0 / jnp.maximum(jnp.abs(rd[:, DV:]), _per_row(rows[4 + d:5 + d, :]))
                hd = rd[:, :DV] * jnp.concatenate([r] * (DV // LANES), axis=1)
                h = hd if h is None else h + hd
            hs.append(h)
        normed = [_head_norm(h, hg_ref[0]) for h in hs]
        for j, y in enumerate(normed):
            s = _chunk_start(g * GROUP + j)
            gate = 1.0 + jnp.tanh(o_ref[0, pl.ds(s, CHUNK), :].astype(F32))
            y_ref[0, pl.ds(s, CHUNK), :] = (y * gate).astype(y_ref.dtype)

    def step(g, carry):
        finish(g, readout(g, weighted(g, scores_of(g))))
        return carry

    lax.fori_loop(0, nc // GROUP, step, 0)


def _mlstm_scan(q, kt, v, o, rows, head_g, bsz, t):
    assert t % (CHUNK * GROUP) == 0, t
    nc = t // CHUNK
    v_spec = pl.BlockSpec((1, t, DV), lambda b, h: (h, b, 0))
    return pl.pallas_call(
        _mlstm_scan_kernel,
        grid=(bsz, HEADS),
        in_specs=[pl.BlockSpec((1, t, DK), lambda b, h: (h, b, 0)),
                  pl.BlockSpec((DK, t), lambda b, h: (h, b)),
                  v_spec, v_spec,
                  pl.BlockSpec((1, 1, 6, t), lambda b, h: (b, h, 0, 0)),
                  pl.BlockSpec((1, 1, DV), lambda b, h: (h, 0, 0))],
        out_specs=v_spec,
        out_shape=jax.ShapeDtypeStruct((HEADS, bsz * t, DV), BF16),
        scratch_shapes=[pltpu.VMEM((2, DK, DVN), F32), pltpu.VMEM((2, nc, DK, DVN), BF16),
                        pltpu.VMEM((2, nc, 8, LANES), F32)],
        compiler_params=_params(2),
        name="mlstm_scan",
    )(q, kt, v, o, rows, head_g)


def _gla_scan_kernel(q_ref, k_ref, v_ref, r_ref, a_ref, w2_ref, ba_ref, hg_ref, y_ref,
                     bc_ref, st_ref, sb_ref):
    t = q_ref.shape[1]
    nc = t // CHUNK
    sub = GLA_SUB
    masks = _causal_masks(CHUNK)
    tri = tuple(m.astype(BF16) for m in masks)

    gate_rows = GATE_GROUP * CHUNK

    def log2_gates(i):
        s = i * gate_rows if isinstance(i, int) else pl.multiple_of(i * gate_rows, gate_rows)
        z = _dot(a_ref[0, pl.ds(s, gate_rows), :].astype(BF16), w2_ref[0]) + ba_ref[0]
        lg = _log_sigmoid(z) * (LOG2E / GLA_TAU)
        hi = lg.astype(BF16)
        return hi, (lg - hi.astype(F32)).astype(BF16)

    def cumulate(i, hi, lo):
        s = i * gate_rows if isinstance(i, int) else pl.multiple_of(i * gate_rows, gate_rows)
        parts = []
        for j in range(GATE_GROUP):
            rs = slice(j * CHUNK, (j + 1) * CHUNK)
            for d in (0, 1):
                cs = slice(d * DK, (d + 1) * DK)
                parts.append(_dot(tri[d], jnp.concatenate([hi[rs, cs], lo[rs, cs]], axis=1)))
        for j in range(GATE_GROUP):
            for d in (0, 1):
                pd = parts[2 * j + d]
                bc_ref[pl.ds(s + j * CHUNK, CHUNK), d * DK:(d + 1) * DK] = pd[:, :DK] + pd[:, DK:]

    def gates(i, carry):
        nxt = log2_gates(i + 1)
        cumulate(i, *carry)
        return nxt

    n_blocks = nc // GATE_GROUP
    cumulate(n_blocks - 1, *lax.fori_loop(0, n_blocks - 1, gates, log2_gates(0)))

    eye = (lax.broadcasted_iota(jnp.int32, (DK, DK), 0)
           == lax.broadcasted_iota(jnp.int32, (DK, DK), 1))

    st_ref[...] = jnp.zeros_like(st_ref)

    def states(i, carry):
        for d in (0, 1):
            c = i if d == 0 else nc - 1 - i
            s = _chunk_start(c)
            kc = k_ref[0, pl.ds(s, CHUNK), :].astype(F32)
            bc = bc_ref[pl.ds(s, CHUNK), d * DK:(d + 1) * DK]
            edge = CHUNK - 1 if d == 0 else 0
            b_last = bc[edge:edge + 1, :]
            kd = kc * jnp.exp2(b_last - bc)
            e_col = jnp.sum(jnp.where(eye, jnp.exp2(b_last), 0.0), axis=1, keepdims=True)
            state = st_ref[d]
            sb_ref[d, c] = state.astype(BF16)
            st_ref[d] = e_col * state + _dot_tn(kd.astype(BF16), v_ref[0, pl.ds(s, CHUNK), :])
        return carry

    lax.fori_loop(0, nc, states, 0, unroll=STATE_UNROLL)

    zeros = jnp.zeros((sub, DK), BF16)

    def scores_of(g):
        starts = [_chunk_start(g * GROUP + j) for j in range(GROUP)]
        scores, q_state = [], []
        for s in starts:
            qc = q_ref[0, pl.ds(s, CHUNK), :].astype(F32)
            kc = k_ref[0, pl.ds(s, CHUNK), :].astype(F32)
            for d in (0, 1):
                bc = bc_ref[pl.ds(s, CHUNK), d * DK:(d + 1) * DK]
                first, second = ((slice(0, sub), slice(sub, CHUNK)) if d == 0
                                 else (slice(sub, CHUNK), slice(0, sub)))
                ref_row = sub - 1 if d == 0 else sub
                r = bc[ref_row:ref_row + 1, :]
                qe = (qc * jnp.exp2(bc)).astype(BF16)
                qg = (qc[second] * jnp.exp2(bc[second] - r)).astype(BF16)
                kf = (kc[first] * jnp.exp2(-bc[first])).astype(BF16)
                kg = (kc * jnp.exp2(r - bc)).astype(BF16)
                q_first = jnp.concatenate([qe[first], zeros], axis=1)
                q_second = jnp.concatenate([zeros, qg], axis=1)
                if d == 0:
                    lhs = jnp.concatenate([q_first, q_second], axis=0)
                    kf_all = jnp.concatenate([kf, zeros], axis=0)
                else:
                    lhs = jnp.concatenate([q_second, q_first], axis=0)
                    kf_all = jnp.concatenate([zeros, kf], axis=0)
                keys = jnp.concatenate([kf_all, kg], axis=1)
                scores.append(_dot_nt(lhs, keys))
                q_state.append(qe)
        return scores, tuple(q_state)

    def masked(scores):
        return tuple(jnp.where(masks[n % 2], sc, 0.0).astype(BF16) for n, sc in enumerate(scores))

    def readout(g, amats, q_state):
        outs = []
        for j in range(GROUP):
            c = g * GROUP + j
            vc = v_ref[0, pl.ds(_chunk_start(c), CHUNK), :]
            o = None
            for d in (0, 1):
                od = _dot(jnp.concatenate([q_state[2 * j + d], amats[2 * j + d]], axis=1),
                          jnp.concatenate([sb_ref[d, c], vc], axis=0))
                o = od if o is None else o + od
            outs.append(o)
        return outs

    def finish(g, outs):
        normed = [_head_norm(o, hg_ref[0]) for o in outs]
        for j, y in enumerate(normed):
            s = _chunk_start(g * GROUP + j)
            rh = r_ref[0, pl.ds(s, CHUNK), :].astype(F32)
            y_ref[0, pl.ds(s, CHUNK), :] = (y * (rh * (1.0 + jnp.tanh(rh)))).astype(y_ref.dtype)

    def step(g, carry):
        scores, q_state = scores_of(g)
        finish(g, readout(g, masked(scores), q_state))
        return carry

    lax.fori_loop(0, nc // GROUP, step, 0)


def _gla_scan(q, k, v, r, a, w2, ba, head_g, bsz, t):
    assert t % (CHUNK * GROUP) == 0 and t % (CHUNK * GATE_GROUP) == 0, t
    nc = t // CHUNK
    qk_spec = pl.BlockSpec((1, t, DK), lambda b, h: (h, b, 0))
    v_spec = pl.BlockSpec((1, t, DV), lambda b, h: (h, b, 0))
    return pl.pallas_call(
        _gla_scan_kernel,
        grid=(bsz, HEADS),
        in_specs=[qk_spec, qk_spec, v_spec, v_spec,
                  pl.BlockSpec((1, t, 2 * GLA_RANK), lambda b, h: (b, 0, 0)),
                  pl.BlockSpec((1, 2 * GLA_RANK, 2 * DK), lambda b, h: (h, 0, 0)),
                  pl.BlockSpec((1, 1, 2 * DK), lambda b, h: (h, 0, 0)),
                  pl.BlockSpec((1, 1, DV), lambda b, h: (h, 0, 0))],
        out_specs=v_spec,
        out_shape=jax.ShapeDtypeStruct((HEADS, bsz * t, DV), BF16),
        scratch_shapes=[pltpu.VMEM((t, 2 * DK), F32),
                        pltpu.VMEM((2, DK, DV), F32), pltpu.VMEM((2, nc, DK, DV), BF16)],
        compiler_params=_params(2),
        name="gla_scan",
    )(q, k, v, r, a, w2, ba, head_g)


def _mix_ffn_kernel(y_ref, x_ref, wo_ref, gn_ref, wg_ref, wu_ref, wd_ref, gl_ref, out_ref,
                    *, final):
    y = jnp.concatenate([y_ref[hd] for hd in range(y_ref.shape[0])], axis=1)
    x1 = x_ref[...] + _dot(y, wo_ref[...])
    h = _rmsnorm(x1, gn_ref[...]).astype(BF16)
    acc = x1
    d_ff = wg_ref.shape[1]
    for c in range(d_ff // FFN_CHUNK):
        cs = slice(c * FFN_CHUNK, (c + 1) * FFN_CHUNK)
        g = _dot(h, wg_ref[:, cs])
        u = _dot(h, wu_ref[:, cs])
        acc = acc + _dot((jax.nn.silu(g) * u).astype(BF16), wd_ref[cs, :])
    if final:
        acc = _rmsnorm(acc, gl_ref[...])
    out_ref[...] = acc


def _mix_ffn(y, x, wo, gn, wg, wu, wd, gl, final):
    n_tok, d = x.shape
    tm = TOKEN_TILE
    const = lambda i: (0, 0)
    resident = lambda w: pl.BlockSpec(w.shape, const, pipeline_mode=pl.Buffered(1))
    tile = pl.BlockSpec((tm, d), lambda i: (i, 0))
    y_tile = pl.BlockSpec((y.shape[0], tm, y.shape[2]), lambda i: (0, i, 0))
    vec = pl.BlockSpec((1, d), const)
    return pl.pallas_call(
        functools.partial(_mix_ffn_kernel, final=final),
        grid=(n_tok // tm,),
        in_specs=[y_tile, tile, resident(wo), vec, resident(wg), resident(wu), resident(wd), vec],
        out_specs=tile,
        out_shape=jax.ShapeDtypeStruct((n_tok, d), F32),
        compiler_params=_params(1),
        name="mix_ffn",
    )(y, x, wo, gn.reshape(1, d), wg, wu, wd, gl.reshape(1, d))


def _mlstm_mixer(x, bsz, t, g_norm, w, b_gate, head_g):
    qk = HEADS * DK
    vv = HEADS * DV
    plan = ((False, ((qk, DK ** -0.5, DK),)), (True, ((qk, 1.0, None), (4 * HEADS, 1.0, None))),
            (False, ((vv, 1.0, DV),)), (False, ((vv, 0.5, DV),)))
    q, kt, gates, v, o_half = _inproj(x, g_norm, w, plan, (BF16, BF16, F32, BF16, BF16))
    stats = _mlstm_gates(gates, _gate_rows(b_gate).reshape(4 * HEADS, 1), bsz, t)
    rows = stats.reshape(bsz, 3, 2, HEADS, t).transpose(0, 3, 1, 2, 4).reshape(bsz, HEADS, 6, t)
    return _mlstm_scan(q, kt, v, o_half, rows, (0.5 * head_g).reshape(HEADS, 1, DV), bsz, t)


def _gla_mixer(x, bsz, t, g_norm, w, w2, ba, head_g):
    qk = HEADS * DK
    vv = HEADS * DV
    plan = tuple((False, ((n, s, hd),)) for n, s, hd in
                 ((qk, DK ** -0.5, DK), (qk, 1.0, DK), (vv, 1.0, DV), (vv, 0.5, DV),
                  (2 * GLA_RANK, 1.0, None)))
    q, k, v, r_half, a = _inproj(x, g_norm, w, plan, (BF16, BF16, BF16, BF16, F32))
    return _gla_scan(q, k, v, r_half, a.reshape(bsz, t, 2 * GLA_RANK), w2, ba,
                     head_g.reshape(HEADS, 1, DV), bsz, t)


def _gate_rows(g):
    g4 = g.reshape((4, HEADS) + g.shape[1:])
    return jnp.concatenate([g4[0], g4[2], g4[1], g4[3]], axis=0)


def _split_cols(w, widths):
    out, s = [], 0
    for n in widths:
        out.append(w[:, s:s + n].astype(BF16))
        s += n
    return out


def _gla_gate_weights(w_a2, b_a):
    wf = w_a2[0].reshape(GLA_RANK, HEADS, DK).transpose(1, 0, 2)
    wb = w_a2[1].reshape(GLA_RANK, HEADS, DK).transpose(1, 0, 2)
    z = jnp.zeros_like(wf)
    w2 = jnp.concatenate([jnp.concatenate([wf, z], axis=2),
                          jnp.concatenate([z, wb], axis=2)], axis=1).astype(BF16)
    ba = jnp.concatenate([b_a[0].reshape(HEADS, 1, DK), b_a[1].reshape(HEADS, 1, DK)], axis=2)
    return w2, ba


def kernel(x_prompt, x_sample, norm_mix_g, norm_ffn_g, norm_final_g, mlstm_w_in, mlstm_b_gate, mlstm_head_g, mlstm_w_out, gla_w_in, gla_w_a2, gla_b_a, gla_head_g, gla_w_out, ffn_w_gu, ffn_w_down):
    depth = norm_mix_g.shape[0]
    d_ff = ffn_w_down.shape[1]
    qk, vv = HEADS * DK, HEADS * DV
    mlstm_w = [_split_cols(w, (qk, qk, vv, vv, 4 * HEADS)) for w in mlstm_w_in]
    mlstm_w = [[wq, jnp.concatenate([wk.T, _gate_rows(wgt.T)], axis=0), wv, wo]
               for wq, wk, wv, wo, wgt in mlstm_w]
    gla_w = [_split_cols(w, (qk, qk, vv, vv, 2 * GLA_RANK)) for w in gla_w_in]
    gla_gate = [_gla_gate_weights(w, b) for w, b in zip(gla_w_a2, gla_b_a)]
    w_out = [mlstm_w_out.astype(BF16), gla_w_out.astype(BF16)]
    w_g = ffn_w_gu[:, :, :d_ff].astype(BF16)
    w_u = ffn_w_gu[:, :, d_ff:].astype(BF16)
    w_d = ffn_w_down.astype(BF16)

    outs = []
    for x0 in (x_prompt, x_sample):
        bsz, t, d = x0.shape
        x = x0.reshape(bsz * t, d)
        for i in range(depth):
            j = i // 2
            if i % 2 == 0:
                y = _mlstm_mixer(x, bsz, t, norm_mix_g[i], mlstm_w[j], mlstm_b_gate[j],
                                 mlstm_head_g[j])
            else:
                y = _gla_mixer(x, bsz, t, norm_mix_g[i], gla_w[j], *gla_gate[j], gla_head_g[j])
            x = _mix_ffn(y, x, w_out[i % 2][j], norm_ffn_g[i],
                         w_g[i], w_u[i], w_d[i], norm_final_g, final=(i == depth - 1))
        outs.append(x.reshape(bsz, t, d))
    return tuple(outs)
```

```python
import functools

import jax
import jax.numpy as jnp
from jax import lax
from jax.experimental import pallas as pl
from jax.experimental.pallas import tpu as pltpu

F32 = jnp.float32
BF16 = jnp.bfloat16

EPS = 1e-6
LOG2E = 1.4426950408889634
HEADS = 4
DK = 128
DV = 256
LANES = 128
DVN = DV + LANES
GLA_RANK = 16
GLA_TAU = 16.0
NEG = -1e30

CHUNK = 128
GLA_SUB = 64
STATE_UNROLL = 8
GROUP = 8
GATE_GROUP = 8
TOKEN_TILE = 512
FFN_CHUNK = 256
VMEM_LIMIT = 56 * 1024 * 1024


def _params(n_parallel):
    return pltpu.CompilerParams(
        dimension_semantics=("parallel",) * n_parallel,
        vmem_limit_bytes=VMEM_LIMIT)


def _rmsnorm(x, g):
    return x * lax.rsqrt(jnp.mean(x * x, axis=-1, keepdims=True) + EPS) * g


def _log_sigmoid(x):
    return jnp.minimum(x, 0.0) - jnp.log(1.0 + jnp.exp(-jnp.abs(x)))


def _dot(a, b):
    return jnp.dot(a, b, preferred_element_type=F32)


def _dot_nt(a, b):
    return lax.dot_general(a, b, (((1,), (1,)), ((), ())), preferred_element_type=F32)


def _dot_tn(a, b):
    return lax.dot_general(a, b, (((0,), (0,)), ((), ())), preferred_element_type=F32)


def _chunk_start(c):
    return c * CHUNK if isinstance(c, int) else pl.multiple_of(c * CHUNK, CHUNK)


def _causal_masks(n):
    ti = lax.broadcasted_iota(jnp.int32, (n, n), 0)
    ji = lax.broadcasted_iota(jnp.int32, (n, n), 1)
    return ji <= ti, ji >= ti


def _inproj_kernel(x_ref, g_ref, *refs, plan):
    w_refs, o_refs = refs[:len(plan)], list(refs[len(plan):])
    h = _rmsnorm(x_ref[...], g_ref[...]).astype(BF16)
    for w_ref, (tr, parts) in zip(w_refs, plan):
        y = _dot_nt(w_ref[...], h) if tr else _dot(h, w_ref[...])
        start = 0
        for width, scale, head_dim in parts:
            o_ref = o_refs.pop(0)
            part = y[start:start + width, :] if tr else y[:, start:start + width]
            if scale != 1.0:
                part = part * scale
            part = part.astype(o_ref.dtype)
            if head_dim is None:
                o_ref[...] = part
            else:
                for hd in range(width // head_dim):
                    o_ref[hd] = part[:, hd * head_dim:(hd + 1) * head_dim]
            start += width


def _inproj(x, g, weights, plan, dtypes):
    n_tok, d = x.shape
    tm = TOKEN_TILE
    const = lambda i: (0, 0)
    out_specs, out_shape = [], []
    dts = list(dtypes)
    for tr, parts in plan:
        for width, _, head_dim in parts:
            dt = dts.pop(0)
            if tr:
                out_specs.append(pl.BlockSpec((width, tm), lambda i: (0, i)))
                out_shape.append(jax.ShapeDtypeStruct((width, n_tok), dt))
            elif head_dim is None:
                out_specs.append(pl.BlockSpec((tm, width), lambda i: (i, 0)))
                out_shape.append(jax.ShapeDtypeStruct((n_tok, width), dt))
            else:
                nh = width // head_dim
                out_specs.append(pl.BlockSpec((nh, tm, head_dim), lambda i: (0, i, 0)))
                out_shape.append(jax.ShapeDtypeStruct((nh, n_tok, head_dim), dt))
    return pl.pallas_call(
        functools.partial(_inproj_kernel, plan=tuple(plan)),
        grid=(n_tok // tm,),
        in_specs=[pl.BlockSpec((tm, d), lambda i: (i, 0)),
                  pl.BlockSpec((1, d), const)]
                 + [pl.BlockSpec(w.shape, const) for w in weights],
        out_specs=out_specs,
        out_shape=out_shape,
        compiler_params=_params(1),
        name="inproj",
    )(x, g.reshape(1, d), *weights)


def _scan_lanes(x, op, fill, reverse):
    t = x.shape[-1]
    lane = lax.broadcasted_iota(jnp.int32, x.shape, 1)
    s = 1
    while s < t:
        if reverse:
            x = op(x, jnp.where(lane < t - s, pltpu.roll(x, t - s, axis=1), fill))
        else:
            x = op(x, jnp.where(lane >= s, pltpu.roll(x, s, axis=1), fill))
        s *= 2
    return x


def _mlstm_gates_kernel(g_ref, bias_ref, out_ref):
    n = 2 * HEADS
    lf = _log_sigmoid(g_ref[n:, :] + bias_ref[n:, :])
    fwd = lax.broadcasted_iota(jnp.int32, lf.shape, 0) < HEADS
    b = jnp.where(fwd, _scan_lanes(lf, jnp.add, 0.0, False),
                  _scan_lanes(lf, jnp.add, 0.0, True))
    a = g_ref[:n, :] + bias_ref[:n, :] - b
    amax = jnp.where(fwd, _scan_lanes(a, jnp.maximum, NEG, False),
                     _scan_lanes(a, jnp.maximum, NEG, True))
    out_ref[0, 0] = a * LOG2E
    out_ref[0, 1] = amax * LOG2E
    out_ref[0, 2] = jnp.exp(-(b + amax))


def _mlstm_gates(g, bias, bsz, t):
    return pl.pallas_call(
        _mlstm_gates_kernel,
        grid=(bsz,),
        in_specs=[pl.BlockSpec((4 * HEADS, t), lambda b: (0, b)),
                  pl.BlockSpec((4 * HEADS, 1), lambda b: (0, 0))],
        out_specs=pl.BlockSpec((1, 3, 2 * HEADS, t), lambda b: (b, 0, 0, 0)),
        out_shape=jax.ShapeDtypeStruct((bsz, 3, 2 * HEADS, t), F32),
        compiler_params=_params(1),
        name="mlstm_gates",
    )(g, bias)


def _head_norm(hs, g):
    mean_sq = _dot((hs * hs).astype(BF16), jnp.full((DV, LANES), 1.0 / DV, BF16))
    inv = lax.rsqrt(mean_sq + EPS)
    return hs * jnp.concatenate([inv] * (DV // LANES), axis=1) * g


def _per_row(x):
    return jnp.broadcast_to(x, (LANES, x.shape[1])).T


def _mlstm_scan_kernel(q_ref, kt_ref, v_ref, o_ref, row_ref, hg_ref, y_ref,
                       st_ref, cb_ref, ap_ref):
    t = q_ref.shape[1]
    nc = t // CHUNK
    masks = _causal_masks(CHUNK)
    ones = jnp.ones((CHUNK, LANES), BF16)

    st_ref[...] = jnp.zeros_like(st_ref)

    def states(i, a_prevs):
        a_ends = []
        for d, a_prev in enumerate(a_prevs):
            c = i if d == 0 else nc - 1 - i
            s = _chunk_start(c)
            kt = kt_ref[:, pl.ds(s, CHUNK)].astype(F32)
            rows = row_ref[0, 0, :, pl.ds(s, CHUNK)]
            edge = CHUNK - 1 if d == 0 else 0
            a_end = rows[2 + d:3 + d, edge:edge + 1]
            kw = kt * jnp.exp2(rows[d:d + 1, :] - a_end)
            va = jnp.concatenate([v_ref[0, pl.ds(s, CHUNK), :], ones], axis=1)
            state = st_ref[d]
            cb_ref[d, c] = state.astype(BF16)
            ap_ref[d, c] = jnp.broadcast_to(a_prev, (8, LANES))
            st_ref[d] = jnp.exp2(a_prev - a_end) * state + _dot(kw.astype(BF16), va)
            a_ends.append(a_end)
        return tuple(a_ends)

    lax.fori_loop(0, nc, states, (jnp.full((1, 1), NEG, F32),) * 2, unroll=STATE_UNROLL)

    def scores_of(g):
        starts = [_chunk_start(g * GROUP + j) for j in range(GROUP)]
        return [_dot(q_ref[0, pl.ds(s, CHUNK), :], kt_ref[:, pl.ds(s, CHUNK)]) for s in starts]

    def weighted(g, scores):
        lhs = []
        for j, sm in enumerate(scores):
            c = g * GROUP + j
            s = _chunk_start(c)
            rows = row_ref[0, 0, :, pl.ds(s, CHUNK)]
            qf = q_ref[0, pl.ds(s, CHUNK), :].astype(F32)
            for d in (0, 1):
                a_row = rows[d:d + 1, :]
                amax = _per_row(rows[2 + d:3 + d, :])
                a_prev = ap_ref[d, c][0:1, 0:1]
                p = sm * jnp.exp2(jnp.where(masks[d], a_row - amax, NEG))
                sq = qf * jnp.exp2(a_prev - amax)
                lhs.append(jnp.concatenate([p.astype(BF16), sq.astype(BF16)], axis=1))
        return tuple(lhs)

    def readout(g, lhs):
        res = []
        for j in range(GROUP):
            c = g * GROUP + j
            va = jnp.concatenate([v_ref[0, pl.ds(_chunk_start(c), CHUNK), :], ones], axis=1)
            for d in (0, 1):
                res.append(_dot(lhs[2 * j + d], jnp.concatenate([va, cb_ref[d, c]], axis=0)))
        return res

    def finish(g, res):
        hs = []
        for j in range(GROUP):
            rows = row_ref[0, 0, :, pl.ds(_chunk_start(g * GROUP + j), CHUNK)]
            h = None
            for d in (0, 1):
                rd = res[2 * j + d]
                r = 1.0 / jnp.maximum(jnp.abs(rd[:, DV:]), _per_row(rows[4 + d:5 + d, :]))
                hd = rd[:, :DV] * jnp.concatenate([r] * (DV // LANES), axis=1)
                h = hd if h is None else h + hd
            hs.append(h)
        normed = [_head_norm(h, hg_ref[0]) for h in hs]
        for j, y in enumerate(normed):
            s = _chunk_start(g * GROUP + j)
            gate = 1.0 + jnp.tanh(o_ref[0, pl.ds(s, CHUNK), :].astype(F32))
            y_ref[0, pl.ds(s, CHUNK), :] = (y * gate).astype(y_ref.dtype)

    def step(g, carry):
        finish(g, readout(g, weighted(g, scores_of(g))))
        return carry

    lax.fori_loop(0, nc // GROUP, step, 0)


def _mlstm_scan(q, kt, v, o, rows, head_g, bsz, t):
    assert t % (CHUNK * GROUP) == 0, t
    nc = t // CHUNK
    v_spec = pl.BlockSpec((1, t, DV), lambda b, h: (h, b, 0))
    return pl.pallas_call(
        _mlstm_scan_kernel,
        grid=(bsz, HEADS),
        in_specs=[pl.BlockSpec((1, t, DK), lambda b, h: (h, b, 0)),
                  pl.BlockSpec((DK, t), lambda b, h: (h, b)),
                  v_spec, v_spec,
                  pl.BlockSpec((1, 1, 6, t), lambda b, h: (b, h, 0, 0)),
                  pl.BlockSpec((1, 1, DV), lambda b, h: (h, 0, 0))],
        out_specs=v_spec,
        out_shape=jax.ShapeDtypeStruct((HEADS, bsz * t, DV), BF16),
        scratch_shapes=[pltpu.VMEM((2, DK, DVN), F32), pltpu.VMEM((2, nc, DK, DVN), BF16),
                        pltpu.VMEM((2, nc, 8, LANES), F32)],
        compiler_params=_params(2),
        name="mlstm_scan",
    )(q, kt, v, o, rows, head_g)


def _gla_scan_kernel(q_ref, k_ref, v_ref, r_ref, a_ref, w2_ref, ba_ref, hg_ref, y_ref,
                     bc_ref, st_ref, sb_ref):
    t = q_ref.shape[1]
    nc = t // CHUNK
    sub = GLA_SUB
    masks = _causal_masks(CHUNK)
    tri = tuple(m.astype(BF16) for m in masks)

    gate_rows = GATE_GROUP * CHUNK

    def log2_gates(i):
        s = i * gate_rows if isinstance(i, int) else pl.multiple_of(i * gate_rows, gate_rows)
        z = _dot(a_ref[0, pl.ds(s, gate_rows), :].astype(BF16), w2_ref[0]) + ba_ref[0]
        lg = _log_sigmoid(z) * (LOG2E / GLA_TAU)
        hi = lg.astype(BF16)
        return hi, (lg - hi.astype(F32)).astype(BF16)

    def cumulate(i, hi, lo):
        s = i * gate_rows if isinstance(i, int) else pl.multiple_of(i * gate_rows, gate_rows)
        parts = []
        for j in range(GATE_GROUP):
            rs = slice(j * CHUNK, (j + 1) * CHUNK)
            for d in (0, 1):
                cs = slice(d * DK, (d + 1) * DK)
                parts.append(_dot(tri[d], jnp.concatenate([hi[rs, cs], lo[rs, cs]], axis=1)))
        for j in range(GATE_GROUP):
            for d in (0, 1):
                pd = parts[2 * j + d]
                bc_ref[pl.ds(s + j * CHUNK, CHUNK), d * DK:(d + 1) * DK] = pd[:, :DK] + pd[:, DK:]

    def gates(i, carry):
        nxt = log2_gates(i + 1)
        cumulate(i, *carry)
        return nxt

    n_blocks = nc // GATE_GROUP
    cumulate(n_blocks - 1, *lax.fori_loop(0, n_blocks - 1, gates, log2_gates(0)))

    st_ref[...] = jnp.zeros_like(st_ref)

    def states(i, carry):
        for d in (0, 1):
            c = i if d == 0 else nc - 1 - i
            s = _chunk_start(c)
            kc = k_ref[0, pl.ds(s, CHUNK), :].astype(F32)
            bc = bc_ref[pl.ds(s, CHUNK), d * DK:(d + 1) * DK]
            edge = CHUNK - 1 if d == 0 else 0
            b_last = bc[edge:edge + 1, :]
            kd = kc * jnp.exp2(b_last - bc)
            decay = jnp.concatenate([_per_row(jnp.exp2(b_last))] * (DV // LANES), axis=1)
            state = st_ref[d]
            sb_ref[d, c] = state.astype(BF16)
            st_ref[d] = decay * state + _dot_tn(kd.astype(BF16), v_ref[0, pl.ds(s, CHUNK), :])
        return carry

    lax.fori_loop(0, nc, states, 0, unroll=STATE_UNROLL)

    zeros = jnp.zeros((sub, DK), BF16)

    def scores_of(g):
        starts = [_chunk_start(g * GROUP + j) for j in range(GROUP)]
        scores, q_state = [], []
        for s in starts:
            qc = q_ref[0, pl.ds(s, CHUNK), :].astype(F32)
            kc = k_ref[0, pl.ds(s, CHUNK), :].astype(F32)
            for d in (0, 1):
                bc = bc_ref[pl.ds(s, CHUNK), d * DK:(d + 1) * DK]
                first, second = ((slice(0, sub), slice(sub, CHUNK)) if d == 0
                                 else (slice(sub, CHUNK), slice(0, sub)))
                ref_row = sub - 1 if d == 0 else sub
                r = bc[ref_row:ref_row + 1, :]
                qe = (qc * jnp.exp2(bc)).astype(BF16)
                qg = (qc[second] * jnp.exp2(bc[second] - r)).astype(BF16)
                kf = (kc[first] * jnp.exp2(-bc[first])).astype(BF16)
                kg = (kc * jnp.exp2(r - bc)).astype(BF16)
                q_first = jnp.concatenate([qe[first], zeros], axis=1)
                q_second = jnp.concatenate([zeros, qg], axis=1)
                if d == 0:
                    lhs = jnp.concatenate([q_first, q_second], axis=0)
                    kf_all = jnp.concatenate([kf, zeros], axis=0)
                else:
                    lhs = jnp.concatenate([q_second, q_first], axis=0)
                    kf_all = jnp.concatenate([zeros, kf], axis=0)
                keys = jnp.concatenate([kf_all, kg], axis=1)
                scores.append(_dot_nt(lhs, keys))
                q_state.append(qe)
        return scores, tuple(q_state)

    def masked(scores):
        return tuple(jnp.where(masks[n % 2], sc, 0.0).astype(BF16) for n, sc in enumerate(scores))

    def readout(g, amats, q_state):
        outs = []
        for j in range(GROUP):
            c = g * GROUP + j
            vc = v_ref[0, pl.ds(_chunk_start(c), CHUNK), :]
            o = None
            for d in (0, 1):
                od = _dot(jnp.concatenate([q_state[2 * j + d], amats[2 * j + d]], axis=1),
                          jnp.concatenate([sb_ref[d, c], vc], axis=0))
                o = od if o is None else o + od
            outs.append(o)
        return outs

    def finish(g, outs):
        normed = [_head_norm(o, hg_ref[0]) for o in outs]
        for j, y in enumerate(normed):
            s = _chunk_start(g * GROUP + j)
            rh = r_ref[0, pl.ds(s, CHUNK), :].astype(F32)
            y_ref[0, pl.ds(s, CHUNK), :] = (y * (rh * (1.0 + jnp.tanh(rh)))).astype(y_ref.dtype)

    def step(g, carry):
        scores, q_state = scores_of(g)
        finish(g, readout(g, masked(scores), q_state))
        return carry

    lax.fori_loop(0, nc // GROUP, step, 0)


def _gla_scan(q, k, v, r, a, w2, ba, head_g, bsz, t):
    assert t % (CHUNK * GROUP) == 0 and t % (CHUNK * GATE_GROUP) == 0, t
    nc = t // CHUNK
    qk_spec = pl.BlockSpec((1, t, DK), lambda b, h: (h, b, 0))
    v_spec = pl.BlockSpec((1, t, DV), lambda b, h: (h, b, 0))
    return pl.pallas_call(
        _gla_scan_kernel,
        grid=(bsz, HEADS),
        in_specs=[qk_spec, qk_spec, v_spec, v_spec,
                  pl.BlockSpec((1, t, 2 * GLA_RANK), lambda b, h: (b, 0, 0)),
                  pl.BlockSpec((1, 2 * GLA_RANK, 2 * DK), lambda b, h: (h, 0, 0)),
                  pl.BlockSpec((1, 1, 2 * DK), lambda b, h: (h, 0, 0)),
                  pl.BlockSpec((1, 1, DV), lambda b, h: (h, 0, 0))],
        out_specs=v_spec,
        out_shape=jax.ShapeDtypeStruct((HEADS, bsz * t, DV), BF16),
        scratch_shapes=[pltpu.VMEM((t, 2 * DK), F32),
                        pltpu.VMEM((2, DK, DV), F32), pltpu.VMEM((2, nc, DK, DV), BF16)],
        compiler_params=_params(2),
        name="gla_scan",
    )(q, k, v, r, a, w2, ba, head_g)


def _mix_ffn_kernel(y_ref, x_ref, wo_ref, gn_ref, wgu_ref, wd_ref, gl_ref, out_ref, *, final):
    y = jnp.concatenate([y_ref[hd] for hd in range(y_ref.shape[0])], axis=1)
    x1 = x_ref[...] + _dot(y, wo_ref[...])
    h = _rmsnorm(x1, gn_ref[...]).astype(BF16)
    acc = x1
    d_ff = wd_ref.shape[0]
    for c in range(d_ff // FFN_CHUNK):
        cs = slice(c * FFN_CHUNK, (c + 1) * FFN_CHUNK)
        g = _dot(h, wgu_ref[:, cs])
        u = _dot(h, wgu_ref[:, d_ff + c * FFN_CHUNK:d_ff + (c + 1) * FFN_CHUNK])
        acc = acc + _dot((jax.nn.silu(g) * u).astype(BF16), wd_ref[cs, :])
    if final:
        acc = _rmsnorm(acc, gl_ref[...])
    out_ref[...] = acc


def _mix_ffn(y, x, wo, gn, wgu, wd, gl, final):
    n_tok, d = x.shape
    tm = TOKEN_TILE
    const = lambda i: (0, 0)
    resident = lambda w: pl.BlockSpec(w.shape, const, pipeline_mode=pl.Buffered(1))
    tile = pl.BlockSpec((tm, d), lambda i: (i, 0))
    y_tile = pl.BlockSpec((y.shape[0], tm, y.shape[2]), lambda i: (0, i, 0))
    vec = pl.BlockSpec((1, d), const)
    return pl.pallas_call(
        functools.partial(_mix_ffn_kernel, final=final),
        grid=(n_tok // tm,),
        in_specs=[y_tile, tile, resident(wo), vec, resident(wgu), resident(wd), vec],
        out_specs=tile,
        out_shape=jax.ShapeDtypeStruct((n_tok, d), F32),
        compiler_params=_params(1),
        name="mix_ffn",
    )(y, x, wo, gn.reshape(1, d), wgu, wd, gl.reshape(1, d))


def _mlstm_mixer(x, bsz, t, g_norm, w, b_gate, head_g):
    qk = HEADS * DK
    vv = HEADS * DV
    plan = ((False, ((qk, DK ** -0.5, DK),)), (True, ((qk, 1.0, None), (4 * HEADS, 1.0, None))),
            (False, ((vv, 1.0, DV),)), (False, ((vv, 0.5, DV),)))
    q, kt, gates, v, o_half = _inproj(x, g_norm, w, plan, (BF16, BF16, F32, BF16, BF16))
    stats = _mlstm_gates(gates, _gate_rows(b_gate).reshape(4 * HEADS, 1), bsz, t)
    rows = stats.reshape(bsz, 3, 2, HEADS, t).transpose(0, 3, 1, 2, 4).reshape(bsz, HEADS, 6, t)
    return _mlstm_scan(q, kt, v, o_half, rows, (0.5 * head_g).reshape(HEADS, 1, DV), bsz, t)


def _gla_mixer(x, bsz, t, g_norm, w, w2, ba, head_g):
    qk = HEADS * DK
    vv = HEADS * DV
    plan = tuple((False, ((n, s, hd),)) for n, s, hd in
                 ((qk, DK ** -0.5, DK), (qk, 1.0, DK), (vv, 1.0, DV), (vv, 0.5, DV),
                  (2 * GLA_RANK, 1.0, None)))
    q, k, v, r_half, a = _inproj(x, g_norm, w, plan, (BF16, BF16, BF16, BF16, F32))
    return _gla_scan(q, k, v, r_half, a.reshape(bsz, t, 2 * GLA_RANK), w2, ba,
                     head_g.reshape(HEADS, 1, DV), bsz, t)


def _gate_rows(g):
    g4 = g.reshape((4, HEADS) + g.shape[1:])
    return jnp.concatenate([g4[0], g4[2], g4[1], g4[3]], axis=0)


def _split_cols(w, widths):
    out, s = [], 0
    for n in widths:
        out.append(w[:, s:s + n].astype(BF16))
        s += n
    return out


def _gla_gate_weights(w_a2, b_a):
    wf = w_a2[0].reshape(GLA_RANK, HEADS, DK).transpose(1, 0, 2)
    wb = w_a2[1].reshape(GLA_RANK, HEADS, DK).transpose(1, 0, 2)
    z = jnp.zeros_like(wf)
    w2 = jnp.concatenate([jnp.concatenate([wf, z], axis=2),
                          jnp.concatenate([z, wb], axis=2)], axis=1).astype(BF16)
    ba = jnp.concatenate([b_a[0].reshape(HEADS, 1, DK), b_a[1].reshape(HEADS, 1, DK)], axis=2)
    return w2, ba


def kernel(x_prompt, x_sample, norm_mix_g, norm_ffn_g, norm_final_g, mlstm_w_in, mlstm_b_gate, mlstm_head_g, mlstm_w_out, gla_w_in, gla_w_a2, gla_b_a, gla_head_g, gla_w_out, ffn_w_gu, ffn_w_down):
    depth = norm_mix_g.shape[0]
    d_ff = ffn_w_down.shape[1]
    qk, vv = HEADS * DK, HEADS * DV
    mlstm_w = [_split_cols(w, (qk, qk, vv, vv, 4 * HEADS)) for w in mlstm_w_in]
    mlstm_w = [[wq, jnp.concatenate([wk.T, _gate_rows(wgt.T)], axis=0), wv, wo]
               for wq, wk, wv, wo, wgt in mlstm_w]
    gla_w = [_split_cols(w, (qk, qk, vv, vv, 2 * GLA_RANK)) for w in gla_w_in]
    gla_gate = [_gla_gate_weights(w, b) for w, b in zip(gla_w_a2, gla_b_a)]
    w_out = [mlstm_w_out.astype(BF16), gla_w_out.astype(BF16)]
    w_gu = ffn_w_gu.astype(BF16)
    w_d = ffn_w_down.astype(BF16)

    outs = []
    for x0 in (x_prompt, x_sample):
        bsz, t, d = x0.shape
        x = x0.reshape(bsz * t, d)
        for i in range(depth):
            j = i // 2
            if i % 2 == 0:
                y = _mlstm_mixer(x, bsz, t, norm_mix_g[i], mlstm_w[j], mlstm_b_gate[j],
                                 mlstm_head_g[j])
            else:
                y = _gla_mixer(x, bsz, t, norm_mix_g[i], gla_w[j], *gla_gate[j], gla_head_g[j])
            x = _mix_ffn(y, x, w_out[i % 2][j], norm_ffn_g[i],
                         w_gu[i], w_d[i], norm_final_g, final=(i == depth - 1))
        outs.append(x.reshape(bsz, t, d))
    return tuple(outs)
```

```python
import functools

import jax
import jax.numpy as jnp
from jax import lax
from jax.experimental import pallas as pl
from jax.experimental.pallas import tpu as pltpu

F32 = jnp.float32
BF16 = jnp.bfloat16

EPS = 1e-6
LOG2E = 1.4426950408889634
HEADS = 4
DK = 128
DV = 256
LANES = 128
DVN = DV + LANES
GLA_RANK = 16
GLA_TAU = 16.0
NEG = -1e30

CHUNK = 128
GLA_SUB = 64
STATE_UNROLL = 8
GROUP = 8
GATE_GROUP = 8
TOKEN_TILE = 1024
FFN_CHUNK = 256
VMEM_LIMIT = 56 * 1024 * 1024


def _params(n_parallel):
    return pltpu.CompilerParams(
        dimension_semantics=("parallel",) * n_parallel,
        vmem_limit_bytes=VMEM_LIMIT)


def _rmsnorm(x, g):
    return x * lax.rsqrt(jnp.mean(x * x, axis=-1, keepdims=True) + EPS) * g


def _log_sigmoid(x):
    return jnp.minimum(x, 0.0) - jnp.log(1.0 + jnp.exp(-jnp.abs(x)))


def _dot(a, b):
    return jnp.dot(a, b, preferred_element_type=F32)


def _dot_nt(a, b):
    return lax.dot_general(a, b, (((1,), (1,)), ((), ())), preferred_element_type=F32)


def _dot_tn(a, b):
    return lax.dot_general(a, b, (((0,), (0,)), ((), ())), preferred_element_type=F32)


def _chunk_start(c):
    return c * CHUNK if isinstance(c, int) else pl.multiple_of(c * CHUNK, CHUNK)


def _causal_masks(n):
    ti = lax.broadcasted_iota(jnp.int32, (n, n), 0)
    ji = lax.broadcasted_iota(jnp.int32, (n, n), 1)
    return ji <= ti, ji >= ti


def _inproj_kernel(x_ref, g_ref, *refs, plan):
    w_refs, o_refs = refs[:len(plan)], list(refs[len(plan):])
    h = _rmsnorm(x_ref[...], g_ref[...]).astype(BF16)
    for w_ref, (tr, parts) in zip(w_refs, plan):
        y = _dot_nt(w_ref[...], h) if tr else _dot(h, w_ref[...])
        start = 0
        for width, scale, head_dim in parts:
            o_ref = o_refs.pop(0)
            part = y[start:start + width, :] if tr else y[:, start:start + width]
            if scale != 1.0:
                part = part * scale
            part = part.astype(o_ref.dtype)
            if head_dim is None:
                o_ref[...] = part
            else:
                for hd in range(width // head_dim):
                    o_ref[hd] = part[:, hd * head_dim:(hd + 1) * head_dim]
            start += width


def _inproj(x, g, weights, plan, dtypes):
    n_tok, d = x.shape
    tm = TOKEN_TILE
    const = lambda i: (0, 0)
    out_specs, out_shape = [], []
    dts = list(dtypes)
    for tr, parts in plan:
        for width, _, head_dim in parts:
            dt = dts.pop(0)
            if tr:
                out_specs.append(pl.BlockSpec((width, tm), lambda i: (0, i)))
                out_shape.append(jax.ShapeDtypeStruct((width, n_tok), dt))
            elif head_dim is None:
                out_specs.append(pl.BlockSpec((tm, width), lambda i: (i, 0)))
                out_shape.append(jax.ShapeDtypeStruct((n_tok, width), dt))
            else:
                nh = width // head_dim
                out_specs.append(pl.BlockSpec((nh, tm, head_dim), lambda i: (0, i, 0)))
                out_shape.append(jax.ShapeDtypeStruct((nh, n_tok, head_dim), dt))
    return pl.pallas_call(
        functools.partial(_inproj_kernel, plan=tuple(plan)),
        grid=(n_tok // tm,),
        in_specs=[pl.BlockSpec((tm, d), lambda i: (i, 0)),
                  pl.BlockSpec((1, d), const)]
                 + [pl.BlockSpec(w.shape, const) for w in weights],
        out_specs=out_specs,
        out_shape=out_shape,
        compiler_params=_params(1),
        name="inproj",
    )(x, g.reshape(1, d), *weights)


def _scan_lanes(x, op, fill, reverse):
    t = x.shape[-1]
    lane = lax.broadcasted_iota(jnp.int32, x.shape, 1)
    s = 1
    while s < t:
        if reverse:
            x = op(x, jnp.where(lane < t - s, pltpu.roll(x, t - s, axis=1), fill))
        else:
            x = op(x, jnp.where(lane >= s, pltpu.roll(x, s, axis=1), fill))
        s *= 2
    return x


def _mlstm_gates_kernel(g_ref, bias_ref, out_ref):
    n = 2 * HEADS
    lf = _log_sigmoid(g_ref[n:, :] + bias_ref[n:, :])
    fwd = lax.broadcasted_iota(jnp.int32, lf.shape, 0) < HEADS
    b = jnp.where(fwd, _scan_lanes(lf, jnp.add, 0.0, False),
                  _scan_lanes(lf, jnp.add, 0.0, True))
    a = g_ref[:n, :] + bias_ref[:n, :] - b
    amax = jnp.where(fwd, _scan_lanes(a, jnp.maximum, NEG, False),
                     _scan_lanes(a, jnp.maximum, NEG, True))
    out_ref[0, 0] = a * LOG2E
    out_ref[0, 1] = amax * LOG2E
    out_ref[0, 2] = jnp.exp(-(b + amax))


def _mlstm_gates(g, bias, bsz, t):
    return pl.pallas_call(
        _mlstm_gates_kernel,
        grid=(bsz,),
        in_specs=[pl.BlockSpec((4 * HEADS, t), lambda b: (0, b)),
                  pl.BlockSpec((4 * HEADS, 1), lambda b: (0, 0))],
        out_specs=pl.BlockSpec((1, 3, 2 * HEADS, t), lambda b: (b, 0, 0, 0)),
        out_shape=jax.ShapeDtypeStruct((bsz, 3, 2 * HEADS, t), F32),
        compiler_params=_params(1),
        name="mlstm_gates",
    )(g, bias)


def _head_norm(hs, g):
    mean_sq = _dot((hs * hs).astype(BF16), jnp.full((DV, LANES), 1.0 / DV, BF16))
    inv = lax.rsqrt(mean_sq + EPS)
    return hs * jnp.concatenate([inv] * (DV // LANES), axis=1) * g


def _per_row(x):
    return jnp.broadcast_to(x, (LANES, x.shape[1])).T


def _mlstm_scan_kernel(q_ref, kt_ref, v_ref, o_ref, row_ref, hg_ref, y_ref,
                       st_ref, cb_ref, ap_ref):
    t = q_ref.shape[1]
    nc = t // CHUNK
    masks = _causal_masks(CHUNK)
    ones = jnp.ones((CHUNK, LANES), BF16)

    st_ref[...] = jnp.zeros_like(st_ref)

    def states(i, a_prevs):
        a_ends = []
        for d, a_prev in enumerate(a_prevs):
            c = i if d == 0 else nc - 1 - i
            s = _chunk_start(c)
            kt = kt_ref[:, pl.ds(s, CHUNK)].astype(F32)
            rows = row_ref[0, 0, :, pl.ds(s, CHUNK)]
            edge = CHUNK - 1 if d == 0 else 0
            a_end = rows[2 + d:3 + d, edge:edge + 1]
            kw = kt * jnp.exp2(rows[d:d + 1, :] - a_end)
            va = jnp.concatenate([v_ref[0, pl.ds(s, CHUNK), :], ones], axis=1)
            state = st_ref[d]
            cb_ref[d, c] = state.astype(BF16)
            ap_ref[d, c] = jnp.broadcast_to(a_prev, (8, LANES))
            st_ref[d] = jnp.exp2(a_prev - a_end) * state + _dot(kw.astype(BF16), va)
            a_ends.append(a_end)
        return tuple(a_ends)

    lax.fori_loop(0, nc, states, (jnp.full((1, 1), NEG, F32),) * 2, unroll=STATE_UNROLL)

    def scores_of(g):
        starts = [_chunk_start(g * GROUP + j) for j in range(GROUP)]
        return [_dot(q_ref[0, pl.ds(s, CHUNK), :], kt_ref[:, pl.ds(s, CHUNK)]) for s in starts]

    def weighted(g, scores):
        lhs = []
        for j, sm in enumerate(scores):
            c = g * GROUP + j
            s = _chunk_start(c)
            rows = row_ref[0, 0, :, pl.ds(s, CHUNK)]
            qf = q_ref[0, pl.ds(s, CHUNK), :].astype(F32)
            for d in (0, 1):
                a_row = rows[d:d + 1, :]
                amax = _per_row(rows[2 + d:3 + d, :])
                a_prev = ap_ref[d, c][0:1, 0:1]
                p = sm * jnp.exp2(jnp.where(masks[d], a_row - amax, NEG))
                sq = qf * jnp.exp2(a_prev - amax)
                lhs.append(jnp.concatenate([p.astype(BF16), sq.astype(BF16)], axis=1))
        return tuple(lhs)

    def readout(g, lhs):
        res = []
        for j in range(GROUP):
            c = g * GROUP + j
            va = jnp.concatenate([v_ref[0, pl.ds(_chunk_start(c), CHUNK), :], ones], axis=1)
            for d in (0, 1):
                res.append(_dot(lhs[2 * j + d], jnp.concatenate([va, cb_ref[d, c]], axis=0)))
        return res

    def finish(g, res):
        hs = []
        for j in range(GROUP):
            rows = row_ref[0, 0, :, pl.ds(_chunk_start(g * GROUP + j), CHUNK)]
            h = None
            for d in (0, 1):
                rd = res[2 * j + d]
                r = 1.0 / jnp.maximum(jnp.abs(rd[:, DV:]), _per_row(rows[4 + d:5 + d, :]))
                hd = rd[:, :DV] * jnp.concatenate([r] * (DV // LANES), axis=1)
                h = hd if h is None else h + hd
            hs.append(h)
        normed = [_head_norm(h, hg_ref[0]) for h in hs]
        for j, y in enumerate(normed):
            s = _chunk_start(g * GROUP + j)
            gate = 1.0 + jnp.tanh(o_ref[0, pl.ds(s, CHUNK), :].astype(F32))
            y_ref[0, pl.ds(s, CHUNK), :] = (y * gate).astype(y_ref.dtype)

    def step(g, carry):
        finish(g, readout(g, weighted(g, scores_of(g))))
        return carry

    lax.fori_loop(0, nc // GROUP, step, 0)


def _mlstm_scan(q, kt, v, o, rows, head_g, bsz, t):
    assert t % (CHUNK * GROUP) == 0, t
    nc = t // CHUNK
    v_spec = pl.BlockSpec((1, t, DV), lambda b, h: (h, b, 0))
    return pl.pallas_call(
        _mlstm_scan_kernel,
        grid=(bsz, HEADS),
        in_specs=[pl.BlockSpec((1, t, DK), lambda b, h: (h, b, 0)),
                  pl.BlockSpec((DK, t), lambda b, h: (h, b)),
                  v_spec, v_spec,
                  pl.BlockSpec((1, 1, 6, t), lambda b, h: (b, h, 0, 0)),
                  pl.BlockSpec((1, 1, DV), lambda b, h: (h, 0, 0))],
        out_specs=v_spec,
        out_shape=jax.ShapeDtypeStruct((HEADS, bsz * t, DV), BF16),
        scratch_shapes=[pltpu.VMEM((2, DK, DVN), F32), pltpu.VMEM((2, nc, DK, DVN), BF16),
                        pltpu.VMEM((2, nc, 8, LANES), F32)],
        compiler_params=_params(2),
        name="mlstm_scan",
    )(q, kt, v, o, rows, head_g)


def _gla_scan_kernel(q_ref, k_ref, v_ref, r_ref, a_ref, w2_ref, ba_ref, hg_ref, y_ref,
                     bc_ref, st_ref, sb_ref):
    t = q_ref.shape[1]
    nc = t // CHUNK
    sub = GLA_SUB
    masks = _causal_masks(CHUNK)
    tri = tuple(m.astype(BF16) for m in masks)

    gate_rows = GATE_GROUP * CHUNK

    def log2_gates(i):
        s = i * gate_rows if isinstance(i, int) else pl.multiple_of(i * gate_rows, gate_rows)
        z = _dot(a_ref[0, pl.ds(s, gate_rows), :].astype(BF16), w2_ref[0]) + ba_ref[0]
        lg = _log_sigmoid(z) * (LOG2E / GLA_TAU)
        hi = lg.astype(BF16)
        return hi, (lg - hi.astype(F32)).astype(BF16)

    def cumulate(i, hi, lo):
        s = i * gate_rows if isinstance(i, int) else pl.multiple_of(i * gate_rows, gate_rows)
        parts = []
        for j in range(GATE_GROUP):
            rs = slice(j * CHUNK, (j + 1) * CHUNK)
            for d in (0, 1):
                cs = slice(d * DK, (d + 1) * DK)
                parts.append(_dot(tri[d], jnp.concatenate([hi[rs, cs], lo[rs, cs]], axis=1)))
        for j in range(GATE_GROUP):
            for d in (0, 1):
                pd = parts[2 * j + d]
                bc_ref[pl.ds(s + j * CHUNK, CHUNK), d * DK:(d + 1) * DK] = pd[:, :DK] + pd[:, DK:]

    def gates(i, carry):
        nxt = log2_gates(i + 1)
        cumulate(i, *carry)
        return nxt

    n_blocks = nc // GATE_GROUP
    cumulate(n_blocks - 1, *lax.fori_loop(0, n_blocks - 1, gates, log2_gates(0)))

    st_ref[...] = jnp.zeros_like(st_ref)

    def states(i, carry):
        for d in (0, 1):
            c = i if d == 0 else nc - 1 - i
            s = _chunk_start(c)
            kc = k_ref[0, pl.ds(s, CHUNK), :].astype(F32)
            bc = bc_ref[pl.ds(s, CHUNK), d * DK:(d + 1) * DK]
            edge = CHUNK - 1 if d == 0 else 0
            b_last = bc[edge:edge + 1, :]
            kd = kc * jnp.exp2(b_last - bc)
            decay = jnp.concatenate([_per_row(jnp.exp2(b_last))] * (DV // LANES), axis=1)
            state = st_ref[d]
            sb_ref[d, c] = state.astype(BF16)
            st_ref[d] = decay * state + _dot_tn(kd.astype(BF16), v_ref[0, pl.ds(s, CHUNK), :])
        return carry

    lax.fori_loop(0, nc, states, 0, unroll=STATE_UNROLL)

    zeros = jnp.zeros((sub, DK), BF16)

    def scores_of(g):
        starts = [_chunk_start(g * GROUP + j) for j in range(GROUP)]
        scores, q_state = [], []
        for s in starts:
            qc = q_ref[0, pl.ds(s, CHUNK), :].astype(F32)
            kc = k_ref[0, pl.ds(s, CHUNK), :].astype(F32)
            for d in (0, 1):
                bc = bc_ref[pl.ds(s, CHUNK), d * DK:(d + 1) * DK]
                first, second = ((slice(0, sub), slice(sub, CHUNK)) if d == 0
                                 else (slice(sub, CHUNK), slice(0, sub)))
                ref_row = sub - 1 if d == 0 else sub
                r = bc[ref_row:ref_row + 1, :]
                qe = (qc * jnp.exp2(bc)).astype(BF16)
                qg = (qc[second] * jnp.exp2(bc[second] - r)).astype(BF16)
                kf = (kc[first] * jnp.exp2(-bc[first])).astype(BF16)
                kg = (kc * jnp.exp2(r - bc)).astype(BF16)
                q_first = jnp.concatenate([qe[first], zeros], axis=1)
                q_second = jnp.concatenate([zeros, qg], axis=1)
                if d == 0:
                    lhs = jnp.concatenate([q_first, q_second], axis=0)
                    kf_all = jnp.concatenate([kf, zeros], axis=0)
                else:
                    lhs = jnp.concatenate([q_second, q_first], axis=0)
                    kf_all = jnp.concatenate([zeros, kf], axis=0)
                keys = jnp.concatenate([kf_all, kg], axis=1)
                scores.append(_dot_nt(lhs, keys))
                q_state.append(qe)
        return scores, tuple(q_state)

    def masked(scores):
        return tuple(jnp.where(masks[n % 2], sc, 0.0).astype(BF16) for n, sc in enumerate(scores))

    def readout(g, amats, q_state):
        outs = []
        for j in range(GROUP):
            c = g * GROUP + j
            vc = v_ref[0, pl.ds(_chunk_start(c), CHUNK), :]
            o = None
            for d in (0, 1):
                od = _dot(jnp.concatenate([q_state[2 * j + d], amats[2 * j + d]], axis=1),
                          jnp.concatenate([sb_ref[d, c], vc], axis=0))
                o = od if o is None else o + od
            outs.append(o)
        return outs

    def finish(g, outs):
        normed = [_head_norm(o, hg_ref[0]) for o in outs]
        for j, y in enumerate(normed):
            s = _chunk_start(g * GROUP + j)
            rh = r_ref[0, pl.ds(s, CHUNK), :].astype(F32)
            y_ref[0, pl.ds(s, CHUNK), :] = (y * (rh * (1.0 + jnp.tanh(rh)))).astype(y_ref.dtype)

    def step(g, carry):
        scores, q_state = scores_of(g)
        finish(g, readout(g, masked(scores), q_state))
        return carry

    lax.fori_loop(0, nc // GROUP, step, 0)


def _gla_scan(q, k, v, r, a, w2, ba, head_g, bsz, t):
    assert t % (CHUNK * GROUP) == 0 and t % (CHUNK * GATE_GROUP) == 0, t
    nc = t // CHUNK
    qk_spec = pl.BlockSpec((1, t, DK), lambda b, h: (h, b, 0))
    v_spec = pl.BlockSpec((1, t, DV), lambda b, h: (h, b, 0))
    return pl.pallas_call(
        _gla_scan_kernel,
        grid=(bsz, HEADS),
        in_specs=[qk_spec, qk_spec, v_spec, v_spec,
                  pl.BlockSpec((1, t, 2 * GLA_RANK), lambda b, h: (b, 0, 0)),
                  pl.BlockSpec((1, 2 * GLA_RANK, 2 * DK), lambda b, h: (h, 0, 0)),
                  pl.BlockSpec((1, 1, 2 * DK), lambda b, h: (h, 0, 0)),
                  pl.BlockSpec((1, 1, DV), lambda b, h: (h, 0, 0))],
        out_specs=v_spec,
        out_shape=jax.ShapeDtypeStruct((HEADS, bsz * t, DV), BF16),
        scratch_shapes=[pltpu.VMEM((t, 2 * DK), F32),
                        pltpu.VMEM((2, DK, DV), F32), pltpu.VMEM((2, nc, DK, DV), BF16)],
        compiler_params=_params(2),
        name="gla_scan",
    )(q, k, v, r, a, w2, ba, head_g)


def _mix_ffn_kernel(y_ref, x_ref, wo_ref, gn_ref, wgu_ref, wd_ref, gl_ref, out_ref, *, final):
    y = jnp.concatenate([y_ref[hd] for hd in range(y_ref.shape[0])], axis=1)
    x1 = x_ref[...] + _dot(y, wo_ref[...])
    h = _rmsnorm(x1, gn_ref[...]).astype(BF16)
    acc = x1
    d_ff = wd_ref.shape[0]
    for c in range(d_ff // FFN_CHUNK):
        cs = slice(c * FFN_CHUNK, (c + 1) * FFN_CHUNK)
        g = _dot(h, wgu_ref[:, cs])
        u = _dot(h, wgu_ref[:, d_ff + c * FFN_CHUNK:d_ff + (c + 1) * FFN_CHUNK])
        acc = acc + _dot((jax.nn.silu(g) * u).astype(BF16), wd_ref[cs, :])
    if final:
        acc = _rmsnorm(acc, gl_ref[...])
    out_ref[...] = acc


def _mix_ffn(y, x, wo, gn, wgu, wd, gl, final):
    n_tok, d = x.shape
    tm = TOKEN_TILE
    const = lambda i: (0, 0)
    resident = lambda w: pl.BlockSpec(w.shape, const, pipeline_mode=pl.Buffered(1))
    tile = pl.BlockSpec((tm, d), lambda i: (i, 0))
    y_tile = pl.BlockSpec((y.shape[0], tm, y.shape[2]), lambda i: (0, i, 0))
    vec = pl.BlockSpec((1, d), const)
    return pl.pallas_call(
        functools.partial(_mix_ffn_kernel, final=final),
        grid=(n_tok // tm,),
        in_specs=[y_tile, tile, resident(wo), vec, resident(wgu), resident(wd), vec],
        out_specs=tile,
        out_shape=jax.ShapeDtypeStruct((n_tok, d), F32),
        compiler_params=_params(1),
        name="mix_ffn",
    )(y, x, wo, gn.reshape(1, d), wgu, wd, gl.reshape(1, d))


def _mlstm_mixer(x, bsz, t, g_norm, w, b_gate, head_g):
    qk = HEADS * DK
    vv = HEADS * DV
    plan = ((False, ((qk, DK ** -0.5, DK),)), (True, ((qk, 1.0, None), (4 * HEADS, 1.0, None))),
            (False, ((vv, 1.0, DV),)), (False, ((vv, 0.5, DV),)))
    q, kt, gates, v, o_half = _inproj(x, g_norm, w, plan, (BF16, BF16, F32, BF16, BF16))
    stats = _mlstm_gates(gates, _gate_rows(b_gate).reshape(4 * HEADS, 1), bsz, t)
    rows = stats.reshape(bsz, 3, 2, HEADS, t).transpose(0, 3, 1, 2, 4).reshape(bsz, HEADS, 6, t)
    return _mlstm_scan(q, kt, v, o_half, rows, (0.5 * head_g).reshape(HEADS, 1, DV), bsz, t)


def _gla_mixer(x, bsz, t, g_norm, w, w2, ba, head_g):
    qk = HEADS * DK
    vv = HEADS * DV
    plan = tuple((False, ((n, s, hd),)) for n, s, hd in
                 ((qk, DK ** -0.5, DK), (qk, 1.0, DK), (vv, 1.0, DV), (vv, 0.5, DV),
                  (2 * GLA_RANK, 1.0, None)))
    q, k, v, r_half, a = _inproj(x, g_norm, w, plan, (BF16, BF16, BF16, BF16, F32))
    return _gla_scan(q, k, v, r_half, a.reshape(bsz, t, 2 * GLA_RANK), w2, ba,
                     head_g.reshape(HEADS, 1, DV), bsz, t)


def _gate_rows(g):
    g4 = g.reshape((4, HEADS) + g.shape[1:])
    return jnp.concatenate([g4[0], g4[2], g4[1], g4[3]], axis=0)


def _split_cols(w, widths):
    out, s = [], 0
    for n in widths:
        out.append(w[:, s:s + n].astype(BF16))
        s += n
    return out


def _gla_gate_weights(w_a2, b_a):
    wf = w_a2[0].reshape(GLA_RANK, HEADS, DK).transpose(1, 0, 2)
    wb = w_a2[1].reshape(GLA_RANK, HEADS, DK).transpose(1, 0, 2)
    z = jnp.zeros_like(wf)
    w2 = jnp.concatenate([jnp.concatenate([wf, z], axis=2),
                          jnp.concatenate([z, wb], axis=2)], axis=1).astype(BF16)
    ba = jnp.concatenate([b_a[0].reshape(HEADS, 1, DK), b_a[1].reshape(HEADS, 1, DK)], axis=2)
    return w2, ba


def kernel(x_prompt, x_sample, norm_mix_g, norm_ffn_g, norm_final_g, mlstm_w_in, mlstm_b_gate, mlstm_head_g, mlstm_w_out, gla_w_in, gla_w_a2, gla_b_a, gla_head_g, gla_w_out, ffn_w_gu, ffn_w_down):
    depth = norm_mix_g.shape[0]
    d_ff = ffn_w_down.shape[1]
    qk, vv = HEADS * DK, HEADS * DV
    mlstm_w = [_split_cols(w, (qk, qk, vv, vv, 4 * HEADS)) for w in mlstm_w_in]
    mlstm_w = [[wq, jnp.concatenate([wk.T, _gate_rows(wgt.T)], axis=0), wv, wo]
               for wq, wk, wv, wo, wgt in mlstm_w]
    gla_w = [_split_cols(w, (qk, qk, vv, vv, 2 * GLA_RANK)) for w in gla_w_in]
    gla_gate = [_gla_gate_weights(w, b) for w, b in zip(gla_w_a2, gla_b_a)]
    w_out = [mlstm_w_out.astype(BF16), gla_w_out.astype(BF16)]
    w_gu = ffn_w_gu.astype(BF16)
    w_d = ffn_w_down.astype(BF16)

    outs = []
    for x0 in (x_prompt, x_sample):
        bsz, t, d = x0.shape
        x = x0.reshape(bsz * t, d)
        for i in range(depth):
            j = i // 2
            if i % 2 == 0:
                y = _mlstm_mixer(x, bsz, t, norm_mix_g[i], mlstm_w[j], mlstm_b_gate[j],
                                 mlstm_head_g[j])
            else:
                y = _gla_mixer(x, bsz, t, norm_mix_g[i], gla_w[j], *gla_gate[j], gla_head_g[j])
            x = _mix_ffn(y, x, w_out[i % 2][j], norm_ffn_g[i],
                         w_gu[i], w_d[i], norm_final_g, final=(i == depth - 1))
        outs.append(x.reshape(bsz, t, d))
    return tuple(outs)
```

```python
import functools

import jax
import jax.numpy as jnp
from jax import lax
from jax.experimental import pallas as pl
from jax.experimental.pallas import tpu as pltpu

F32 = jnp.float32
BF16 = jnp.bfloat16

EPS = 1e-6
LOG2E = 1.4426950408889634
HEADS = 4
DK = 128
DV = 256
LANES = 128
DVN = DV + LANES
GLA_RANK = 16
GLA_TAU = 16.0
NEG = -1e30

CHUNK = 128
GLA_SUB = 64
STATE_UNROLL = 8
GROUP = 8
GATE_GROUP = 8
TOKEN_TILE = 1024
FFN_CHUNK = 256
VMEM_LIMIT = 56 * 1024 * 1024


def _params(n_parallel):
    return pltpu.CompilerParams(
        dimension_semantics=("parallel",) * n_parallel,
        vmem_limit_bytes=VMEM_LIMIT)


def _rmsnorm(x, g):
    return x * lax.rsqrt(jnp.mean(x * x, axis=-1, keepdims=True) + EPS) * g


def _log_sigmoid(x):
    return jnp.minimum(x, 0.0) - jnp.log(1.0 + jnp.exp(-jnp.abs(x)))


def _dot(a, b):
    return jnp.dot(a, b, preferred_element_type=F32)


def _dot_nt(a, b):
    return lax.dot_general(a, b, (((1,), (1,)), ((), ())), preferred_element_type=F32)


def _dot_tn(a, b):
    return lax.dot_general(a, b, (((0,), (0,)), ((), ())), preferred_element_type=F32)


def _chunk_start(c):
    return c * CHUNK if isinstance(c, int) else pl.multiple_of(c * CHUNK, CHUNK)


def _causal_masks(n):
    ti = lax.broadcasted_iota(jnp.int32, (n, n), 0)
    ji = lax.broadcasted_iota(jnp.int32, (n, n), 1)
    return ji <= ti, ji >= ti


def _inproj_kernel(x_ref, g_ref, *refs, plan):
    w_refs, o_refs = refs[:len(plan)], list(refs[len(plan):])
    h = _rmsnorm(x_ref[...], g_ref[...]).astype(BF16)
    for w_ref, (tr, parts) in zip(w_refs, plan):
        y = _dot_nt(w_ref[...], h) if tr else _dot(h, w_ref[...])
        start = 0
        for width, scale, head_dim in parts:
            o_ref = o_refs.pop(0)
            part = y[start:start + width, :] if tr else y[:, start:start + width]
            if scale != 1.0:
                part = part * scale
            part = part.astype(o_ref.dtype)
            if head_dim is None:
                o_ref[...] = part
            else:
                for hd in range(width // head_dim):
                    o_ref[hd] = part[:, hd * head_dim:(hd + 1) * head_dim]
            start += width


def _inproj(x, g, weights, plan, dtypes):
    n_tok, d = x.shape
    tm = TOKEN_TILE
    const = lambda i: (0, 0)
    out_specs, out_shape = [], []
    dts = list(dtypes)
    for tr, parts in plan:
        for width, _, head_dim in parts:
            dt = dts.pop(0)
            if tr:
                out_specs.append(pl.BlockSpec((width, tm), lambda i: (0, i)))
                out_shape.append(jax.ShapeDtypeStruct((width, n_tok), dt))
            elif head_dim is None:
                out_specs.append(pl.BlockSpec((tm, width), lambda i: (i, 0)))
                out_shape.append(jax.ShapeDtypeStruct((n_tok, width), dt))
            else:
                nh = width // head_dim
                out_specs.append(pl.BlockSpec((nh, tm, head_dim), lambda i: (0, i, 0)))
                out_shape.append(jax.ShapeDtypeStruct((nh, n_tok, head_dim), dt))
    return pl.pallas_call(
        functools.partial(_inproj_kernel, plan=tuple(plan)),
        grid=(n_tok // tm,),
        in_specs=[pl.BlockSpec((tm, d), lambda i: (i, 0)),
                  pl.BlockSpec((1, d), const)]
                 + [pl.BlockSpec(w.shape, const) for w in weights],
        out_specs=out_specs,
        out_shape=out_shape,
        compiler_params=_params(1),
        name="inproj",
    )(x, g.reshape(1, d), *weights)


def _scan_lanes(x, op, fill, reverse):
    t = x.shape[-1]
    lane = lax.broadcasted_iota(jnp.int32, x.shape, 1)
    s = 1
    while s < t:
        if reverse:
            x = op(x, jnp.where(lane < t - s, pltpu.roll(x, t - s, axis=1), fill))
        else:
            x = op(x, jnp.where(lane >= s, pltpu.roll(x, s, axis=1), fill))
        s *= 2
    return x


def _mlstm_gates_kernel(g_ref, bias_ref, out_ref):
    n = 2 * HEADS
    lf = _log_sigmoid(g_ref[n:, :] + bias_ref[n:, :])
    fwd = lax.broadcasted_iota(jnp.int32, lf.shape, 0) < HEADS
    b = jnp.where(fwd, _scan_lanes(lf, jnp.add, 0.0, False),
                  _scan_lanes(lf, jnp.add, 0.0, True))
    a = g_ref[:n, :] + bias_ref[:n, :] - b
    amax = jnp.where(fwd, _scan_lanes(a, jnp.maximum, NEG, False),
                     _scan_lanes(a, jnp.maximum, NEG, True))
    out_ref[0, 0] = a * LOG2E
    out_ref[0, 1] = amax * LOG2E
    out_ref[0, 2] = jnp.exp(-(b + amax))


def _mlstm_gates(g, bias, bsz, t):
    return pl.pallas_call(
        _mlstm_gates_kernel,
        grid=(bsz,),
        in_specs=[pl.BlockSpec((4 * HEADS, t), lambda b: (0, b)),
                  pl.BlockSpec((4 * HEADS, 1), lambda b: (0, 0))],
        out_specs=pl.BlockSpec((1, 3, 2 * HEADS, t), lambda b: (b, 0, 0, 0)),
        out_shape=jax.ShapeDtypeStruct((bsz, 3, 2 * HEADS, t), F32),
        compiler_params=_params(1),
        name="mlstm_gates",
    )(g, bias)


def _per_row(x):
    return jnp.broadcast_to(x, (LANES, x.shape[1])).T


def _mlstm_scan_kernel(q_ref, kt_ref, v_ref, row_ref, y_ref,
                       st_ref, cb_ref, ap_ref):
    t = q_ref.shape[1]
    nc = t // CHUNK
    masks = _causal_masks(CHUNK)
    ones = jnp.ones((CHUNK, LANES), BF16)

    st_ref[...] = jnp.zeros_like(st_ref)

    def states(i, a_prevs):
        a_ends = []
        for d, a_prev in enumerate(a_prevs):
            c = i if d == 0 else nc - 1 - i
            s = _chunk_start(c)
            kt = kt_ref[:, pl.ds(s, CHUNK)].astype(F32)
            rows = row_ref[0, 0, :, pl.ds(s, CHUNK)]
            edge = CHUNK - 1 if d == 0 else 0
            a_end = rows[2 + d:3 + d, edge:edge + 1]
            kw = kt * jnp.exp2(rows[d:d + 1, :] - a_end)
            va = jnp.concatenate([v_ref[0, pl.ds(s, CHUNK), :], ones], axis=1)
            state = st_ref[d]
            cb_ref[d, c] = state.astype(BF16)
            ap_ref[d, c] = jnp.broadcast_to(a_prev, (8, LANES))
            st_ref[d] = jnp.exp2(a_prev - a_end) * state + _dot(kw.astype(BF16), va)
            a_ends.append(a_end)
        return tuple(a_ends)

    lax.fori_loop(0, nc, states, (jnp.full((1, 1), NEG, F32),) * 2, unroll=STATE_UNROLL)

    def scores_of(g):
        starts = [_chunk_start(g * GROUP + j) for j in range(GROUP)]
        return [_dot(q_ref[0, pl.ds(s, CHUNK), :], kt_ref[:, pl.ds(s, CHUNK)]) for s in starts]

    def weighted(g, scores):
        lhs = []
        for j, sm in enumerate(scores):
            c = g * GROUP + j
            s = _chunk_start(c)
            rows = row_ref[0, 0, :, pl.ds(s, CHUNK)]
            qf = q_ref[0, pl.ds(s, CHUNK), :].astype(F32)
            for d in (0, 1):
                a_row = rows[d:d + 1, :]
                amax = _per_row(rows[2 + d:3 + d, :])
                a_prev = ap_ref[d, c][0:1, 0:1]
                p = sm * jnp.exp2(jnp.where(masks[d], a_row - amax, NEG))
                sq = qf * jnp.exp2(a_prev - amax)
                lhs.append(jnp.concatenate([p.astype(BF16), sq.astype(BF16)], axis=1))
        return tuple(lhs)

    def readout(g, lhs):
        res = []
        for j in range(GROUP):
            c = g * GROUP + j
            va = jnp.concatenate([v_ref[0, pl.ds(_chunk_start(c), CHUNK), :], ones], axis=1)
            for d in (0, 1):
                res.append(_dot(lhs[2 * j + d], jnp.concatenate([va, cb_ref[d, c]], axis=0)))
        return res

    def finish(g, res):
        for j in range(GROUP):
            s = _chunk_start(g * GROUP + j)
            rows = row_ref[0, 0, :, pl.ds(s, CHUNK)]
            h = None
            for d in (0, 1):
                rd = res[2 * j + d]
                r = 1.0 / jnp.maximum(jnp.abs(rd[:, DV:]), _per_row(rows[4 + d:5 + d, :]))
                hd = rd[:, :DV] * jnp.concatenate([r] * (DV // LANES), axis=1)
                h = hd if h is None else h + hd
            y_ref[0, pl.ds(s, CHUNK), :] = h.astype(y_ref.dtype)

    def step(g, carry):
        finish(g, readout(g, weighted(g, scores_of(g))))
        return carry

    lax.fori_loop(0, nc // GROUP, step, 0)


def _mlstm_scan(q, kt, v, rows, bsz, t):
    assert t % (CHUNK * GROUP) == 0, t
    nc = t // CHUNK
    v_spec = pl.BlockSpec((1, t, DV), lambda b, h: (h, b, 0))
    return pl.pallas_call(
        _mlstm_scan_kernel,
        grid=(bsz, HEADS),
        in_specs=[pl.BlockSpec((1, t, DK), lambda b, h: (h, b, 0)),
                  pl.BlockSpec((DK, t), lambda b, h: (h, b)),
                  v_spec,
                  pl.BlockSpec((1, 1, 6, t), lambda b, h: (b, h, 0, 0))],
        out_specs=v_spec,
        out_shape=jax.ShapeDtypeStruct((HEADS, bsz * t, DV), BF16),
        scratch_shapes=[pltpu.VMEM((2, DK, DVN), F32), pltpu.VMEM((2, nc, DK, DVN), BF16),
                        pltpu.VMEM((2, nc, 8, LANES), F32)],
        compiler_params=_params(2),
        name="mlstm_scan",
    )(q, kt, v, rows)


def _gla_scan_kernel(q_ref, k_ref, v_ref, a_ref, w2_ref, ba_ref, y_ref,
                     bc_ref, st_ref, sb_ref):
    t = q_ref.shape[1]
    nc = t // CHUNK
    sub = GLA_SUB
    masks = _causal_masks(CHUNK)
    tri = tuple(m.astype(BF16) for m in masks)

    gate_rows = GATE_GROUP * CHUNK

    def log2_gates(i):
        s = i * gate_rows if isinstance(i, int) else pl.multiple_of(i * gate_rows, gate_rows)
        z = _dot(a_ref[0, pl.ds(s, gate_rows), :].astype(BF16), w2_ref[0]) + ba_ref[0]
        lg = _log_sigmoid(z) * (LOG2E / GLA_TAU)
        hi = lg.astype(BF16)
        return hi, (lg - hi.astype(F32)).astype(BF16)

    def cumulate(i, hi, lo):
        s = i * gate_rows if isinstance(i, int) else pl.multiple_of(i * gate_rows, gate_rows)
        parts = []
        for j in range(GATE_GROUP):
            rs = slice(j * CHUNK, (j + 1) * CHUNK)
            for d in (0, 1):
                cs = slice(d * DK, (d + 1) * DK)
                parts.append(_dot(tri[d], jnp.concatenate([hi[rs, cs], lo[rs, cs]], axis=1)))
        for j in range(GATE_GROUP):
            for d in (0, 1):
                pd = parts[2 * j + d]
                bc_ref[pl.ds(s + j * CHUNK, CHUNK), d * DK:(d + 1) * DK] = pd[:, :DK] + pd[:, DK:]

    def gates(i, carry):
        nxt = log2_gates(i + 1)
        cumulate(i, *carry)
        return nxt

    n_blocks = nc // GATE_GROUP
    cumulate(n_blocks - 1, *lax.fori_loop(0, n_blocks - 1, gates, log2_gates(0)))

    st_ref[...] = jnp.zeros_like(st_ref)

    def states(i, carry):
        for d in (0, 1):
            c = i if d == 0 else nc - 1 - i
            s = _chunk_start(c)
            kc = k_ref[0, pl.ds(s, CHUNK), :].astype(F32)
            bc = bc_ref[pl.ds(s, CHUNK), d * DK:(d + 1) * DK]
            edge = CHUNK - 1 if d == 0 else 0
            b_last = bc[edge:edge + 1, :]
            kd = kc * jnp.exp2(b_last - bc)
            decay = jnp.concatenate([_per_row(jnp.exp2(b_last))] * (DV // LANES), axis=1)
            state = st_ref[d]
            sb_ref[d, c] = state.astype(BF16)
            st_ref[d] = decay * state + _dot_tn(kd.astype(BF16), v_ref[0, pl.ds(s, CHUNK), :])
        return carry

    lax.fori_loop(0, nc, states, 0, unroll=STATE_UNROLL)

    zeros = jnp.zeros((sub, DK), BF16)

    def scores_of(g):
        starts = [_chunk_start(g * GROUP + j) for j in range(GROUP)]
        scores, q_state = [], []
        for s in starts:
            qc = q_ref[0, pl.ds(s, CHUNK), :].astype(F32)
            kc = k_ref[0, pl.ds(s, CHUNK), :].astype(F32)
            for d in (0, 1):
                bc = bc_ref[pl.ds(s, CHUNK), d * DK:(d + 1) * DK]
                first, second = ((slice(0, sub), slice(sub, CHUNK)) if d == 0
                                 else (slice(sub, CHUNK), slice(0, sub)))
                ref_row = sub - 1 if d == 0 else sub
                r = bc[ref_row:ref_row + 1, :]
                qe = (qc * jnp.exp2(bc)).astype(BF16)
                qg = (qc[second] * jnp.exp2(bc[second] - r)).astype(BF16)
                kf = (kc[first] * jnp.exp2(-bc[first])).astype(BF16)
                kg = (kc * jnp.exp2(r - bc)).astype(BF16)
                q_first = jnp.concatenate([qe[first], zeros], axis=1)
                q_second = jnp.concatenate([zeros, qg], axis=1)
                if d == 0:
                    lhs = jnp.concatenate([q_first, q_second], axis=0)
                    kf_all = jnp.concatenate([kf, zeros], axis=0)
                else:
                    lhs = jnp.concatenate([q_second, q_first], axis=0)
                    kf_all = jnp.concatenate([zeros, kf], axis=0)
                keys = jnp.concatenate([kf_all, kg], axis=1)
                scores.append(_dot_nt(lhs, keys))
                q_state.append(qe)
        return scores, tuple(q_state)

    def masked(scores):
        return tuple(jnp.where(masks[n % 2], sc, 0.0).astype(BF16) for n, sc in enumerate(scores))

    def readout(g, amats, q_state):
        outs = []
        for j in range(GROUP):
            c = g * GROUP + j
            vc = v_ref[0, pl.ds(_chunk_start(c), CHUNK), :]
            o = None
            for d in (0, 1):
                od = _dot(jnp.concatenate([q_state[2 * j + d], amats[2 * j + d]], axis=1),
                          jnp.concatenate([sb_ref[d, c], vc], axis=0))
                o = od if o is None else o + od
            outs.append(o)
        return outs

    def step(g, carry):
        scores, q_state = scores_of(g)
        for j, o in enumerate(readout(g, masked(scores), q_state)):
            y_ref[0, pl.ds(_chunk_start(g * GROUP + j), CHUNK), :] = o.astype(y_ref.dtype)
        return carry

    lax.fori_loop(0, nc // GROUP, step, 0)


def _gla_scan(q, k, v, a, w2, ba, bsz, t):
    assert t % (CHUNK * GROUP) == 0 and t % (CHUNK * GATE_GROUP) == 0, t
    nc = t // CHUNK
    qk_spec = pl.BlockSpec((1, t, DK), lambda b, h: (h, b, 0))
    v_spec = pl.BlockSpec((1, t, DV), lambda b, h: (h, b, 0))
    return pl.pallas_call(
        _gla_scan_kernel,
        grid=(bsz, HEADS),
        in_specs=[qk_spec, qk_spec, v_spec,
                  pl.BlockSpec((1, t, 2 * GLA_RANK), lambda b, h: (b, 0, 0)),
                  pl.BlockSpec((1, 2 * GLA_RANK, 2 * DK), lambda b, h: (h, 0, 0)),
                  pl.BlockSpec((1, 1, 2 * DK), lambda b, h: (h, 0, 0))],
        out_specs=v_spec,
        out_shape=jax.ShapeDtypeStruct((HEADS, bsz * t, DV), BF16),
        scratch_shapes=[pltpu.VMEM((t, 2 * DK), F32),
                        pltpu.VMEM((2, DK, DV), F32), pltpu.VMEM((2, nc, DK, DV), BF16)],
        compiler_params=_params(2),
        name="gla_scan",
    )(q, k, v, a, w2, ba)


def _mix_ffn_kernel(hs_ref, gate_ref, hg_ref, x_ref, wo_ref, gn_ref, wgu_ref, wd_ref, gl_ref, out_ref,
                    *, final, silu_gate):
    heads = []
    for hd in range(hs_ref.shape[0]):
        gh = gate_ref[hd].astype(F32)
        gate = 1.0 + jnp.tanh(gh)
        if silu_gate:
            gate = gh * gate
        heads.append((_rmsnorm(hs_ref[hd].astype(F32), hg_ref[hd]) * gate).astype(BF16))
    x1 = x_ref[...] + _dot(jnp.concatenate(heads, axis=1), wo_ref[...])
    h = _rmsnorm(x1, gn_ref[...]).astype(BF16)
    acc = x1
    d_ff = wd_ref.shape[0]
    for c in range(d_ff // FFN_CHUNK):
        cs = slice(c * FFN_CHUNK, (c + 1) * FFN_CHUNK)
        g = _dot(h, wgu_ref[:, cs])
        u = _dot(h, wgu_ref[:, d_ff + c * FFN_CHUNK:d_ff + (c + 1) * FFN_CHUNK])
        acc = acc + _dot((jax.nn.silu(g) * u).astype(BF16), wd_ref[cs, :])
    if final:
        acc = _rmsnorm(acc, gl_ref[...])
    out_ref[...] = acc


def _mix_ffn(hs, gate, head_g, x, wo, gn, wgu, wd, gl, final, silu_gate):
    n_tok, d = x.shape
    tm = TOKEN_TILE
    const = lambda i: (0, 0)
    resident = lambda w: pl.BlockSpec(w.shape, const, pipeline_mode=pl.Buffered(1))
    tile = pl.BlockSpec((tm, d), lambda i: (i, 0))
    head_tile = pl.BlockSpec((hs.shape[0], tm, hs.shape[2]), lambda i: (0, i, 0))
    vec = pl.BlockSpec((1, d), const)
    return pl.pallas_call(
        functools.partial(_mix_ffn_kernel, final=final, silu_gate=silu_gate),
        grid=(n_tok // tm,),
        in_specs=[head_tile, head_tile, pl.BlockSpec(head_g.shape, lambda i: (0, 0, 0)), tile,
                  resident(wo), vec, resident(wgu), resident(wd), vec],
        out_specs=tile,
        out_shape=jax.ShapeDtypeStruct((n_tok, d), F32),
        compiler_params=_params(1),
        name="mix_ffn",
    )(hs, gate, head_g, x, wo, gn.reshape(1, d), wgu, wd, gl.reshape(1, d))


def _mlstm_mixer(x, bsz, t, g_norm, w, b_gate, head_g):
    qk = HEADS * DK
    vv = HEADS * DV
    plan = ((False, ((qk, DK ** -0.5, DK),)), (True, ((qk, 1.0, None), (4 * HEADS, 1.0, None))),
            (False, ((vv, 1.0, DV),)), (False, ((vv, 0.5, DV),)))
    q, kt, gates, v, o_half = _inproj(x, g_norm, w, plan, (BF16, BF16, F32, BF16, BF16))
    stats = _mlstm_gates(gates, _gate_rows(b_gate).reshape(4 * HEADS, 1), bsz, t)
    rows = stats.reshape(bsz, 3, 2, HEADS, t).transpose(0, 3, 1, 2, 4).reshape(bsz, HEADS, 6, t)
    return _mlstm_scan(q, kt, v, rows, bsz, t), o_half, (0.5 * head_g).reshape(HEADS, 1, DV)


def _gla_mixer(x, bsz, t, g_norm, w, w2, ba, head_g):
    qk = HEADS * DK
    vv = HEADS * DV
    plan = tuple((False, ((n, s, hd),)) for n, s, hd in
                 ((qk, DK ** -0.5, DK), (qk, 1.0, DK), (vv, 1.0, DV), (vv, 0.5, DV),
                  (2 * GLA_RANK, 1.0, None)))
    q, k, v, r_half, a = _inproj(x, g_norm, w, plan, (BF16, BF16, BF16, BF16, F32))
    hs = _gla_scan(q, k, v, a.reshape(bsz, t, 2 * GLA_RANK), w2, ba, bsz, t)
    return hs, r_half, head_g.reshape(HEADS, 1, DV)


def _gate_rows(g):
    g4 = g.reshape((4, HEADS) + g.shape[1:])
    return jnp.concatenate([g4[0], g4[2], g4[1], g4[3]], axis=0)


def _split_cols(w, widths):
    out, s = [], 0
    for n in widths:
        out.append(w[:, s:s + n].astype(BF16))
        s += n
    return out


def _gla_gate_weights(w_a2, b_a):
    wf = w_a2[0].reshape(GLA_RANK, HEADS, DK).transpose(1, 0, 2)
    wb = w_a2[1].reshape(GLA_RANK, HEADS, DK).transpose(1, 0, 2)
    z = jnp.zeros_like(wf)
    w2 = jnp.concatenate([jnp.concatenate([wf, z], axis=2),
                          jnp.concatenate([z, wb], axis=2)], axis=1).astype(BF16)
    ba = jnp.concatenate([b_a[0].reshape(HEADS, 1, DK), b_a[1].reshape(HEADS, 1, DK)], axis=2)
    return w2, ba


def kernel(x_prompt, x_sample, norm_mix_g, norm_ffn_g, norm_final_g, mlstm_w_in, mlstm_b_gate, mlstm_head_g, mlstm_w_out, gla_w_in, gla_w_a2, gla_b_a, gla_head_g, gla_w_out, ffn_w_gu, ffn_w_down):
    depth = norm_mix_g.shape[0]
    d_ff = ffn_w_down.shape[1]
    qk, vv = HEADS * DK, HEADS * DV
    mlstm_w = [_split_cols(w, (qk, qk, vv, vv, 4 * HEADS)) for w in mlstm_w_in]
    mlstm_w = [[wq, jnp.concatenate([wk.T, _gate_rows(wgt.T)], axis=0), wv, wo]
               for wq, wk, wv, wo, wgt in mlstm_w]
    gla_w = [_split_cols(w, (qk, qk, vv, vv, 2 * GLA_RANK)) for w in gla_w_in]
    gla_gate = [_gla_gate_weights(w, b) for w, b in zip(gla_w_a2, gla_b_a)]
    w_out = [mlstm_w_out.astype(BF16), gla_w_out.astype(BF16)]
    w_gu = ffn_w_gu.astype(BF16)
    w_d = ffn_w_down.astype(BF16)

    outs = []
    for x0 in (x_prompt, x_sample):
        bsz, t, d = x0.shape
        x = x0.reshape(bsz * t, d)
        for i in range(depth):
            j = i // 2
            if i % 2 == 0:
                mixed = _mlstm_mixer(x, bsz, t, norm_mix_g[i], mlstm_w[j], mlstm_b_gate[j],
                                     mlstm_head_g[j])
            else:
                mixed = _gla_mixer(x, bsz, t, norm_mix_g[i], gla_w[j], *gla_gate[j], gla_head_g[j])
            x = _mix_ffn(*mixed, x, w_out[i % 2][j], norm_ffn_g[i], w_gu[i], w_d[i], norm_final_g,
                         final=(i == depth - 1), silu_gate=(i % 2 == 1))
        outs.append(x.reshape(bsz, t, d))
    return tuple(outs)
```

```python
import functools

import jax
import jax.numpy as jnp
from jax import lax
from jax.experimental import pallas as pl
from jax.experimental.pallas import tpu as pltpu

F32 = jnp.float32
BF16 = jnp.bfloat16

EPS = 1e-6
LOG2E = 1.4426950408889634
HEADS = 4
DK = 128
DV = 256
LANES = 128
DVN = DV + LANES
GLA_RANK = 16
GLA_TAU = 16.0
NEG = -1e30

CHUNK = 128
GLA_SUB = 64
STATE_UNROLL = 8
GROUP = 8
GATE_GROUP = 8
TOKEN_TILE = 1024
FFN_CHUNK = 256
VMEM_LIMIT = 56 * 1024 * 1024


def _params(n_parallel):
    return pltpu.CompilerParams(
        dimension_semantics=("parallel",) * n_parallel,
        vmem_limit_bytes=VMEM_LIMIT)


def _rmsnorm(x, g):
    return x * lax.rsqrt(jnp.mean(x * x, axis=-1, keepdims=True) + EPS) * g


def _log_sigmoid(x):
    return jnp.minimum(x, 0.0) - jnp.log(1.0 + jnp.exp(-jnp.abs(x)))


def _dot(a, b):
    return jnp.dot(a, b, preferred_element_type=F32)


def _dot_nt(a, b):
    return lax.dot_general(a, b, (((1,), (1,)), ((), ())), preferred_element_type=F32)


def _dot_tn(a, b):
    return lax.dot_general(a, b, (((0,), (0,)), ((), ())), preferred_element_type=F32)


def _chunk_start(c):
    return c * CHUNK if isinstance(c, int) else pl.multiple_of(c * CHUNK, CHUNK)


def _causal_masks(n):
    ti = lax.broadcasted_iota(jnp.int32, (n, n), 0)
    ji = lax.broadcasted_iota(jnp.int32, (n, n), 1)
    return ji <= ti, ji >= ti


GLA_GATE = "gla_gate"
GATE_DELAY = 1


def _write_heads(o_ref, value, head_dim):
    for hd in range(value.shape[1] // head_dim):
        o_ref[hd] = value[:, hd * head_dim:(hd + 1) * head_dim]


def _inproj_kernel(x_ref, g_ref, *refs, plan, n_gate_refs):
    w_refs = refs[:len(plan)]
    gate_refs = list(refs[len(plan):len(plan) + n_gate_refs])
    o_refs = list(refs[len(plan) + n_gate_refs:])
    h = _rmsnorm(x_ref[...], g_ref[...]).astype(BF16)
    pending = []

    def gate_stage(low_rank, hi_ref, lo_ref):
        w2_ref, ba_ref = gate_refs.pop(0), gate_refs.pop(0)
        z = _dot(low_rank.astype(BF16), w2_ref[...]) + ba_ref[...]
        lg = _log_sigmoid(z) * (LOG2E / GLA_TAU)
        hi = lg.astype(BF16)
        _write_heads(hi_ref, hi, 2 * DK)
        _write_heads(lo_ref, (lg - hi.astype(F32)).astype(BF16), 2 * DK)

    for n, (w_ref, (tr, parts)) in enumerate(zip(w_refs, plan)):
        y = _dot_nt(w_ref[...], h) if tr else _dot(h, w_ref[...])
        start = 0
        for width, scale, head_dim in parts:
            part = y[start:start + width, :] if tr else y[:, start:start + width]
            start += width
            if scale != 1.0:
                part = part * scale
            if head_dim == GLA_GATE:
                pending.append((n + GATE_DELAY, part, o_refs.pop(0), o_refs.pop(0)))
                continue
            o_ref = o_refs.pop(0)
            part = part.astype(o_ref.dtype)
            if head_dim is None:
                o_ref[...] = part
            else:
                _write_heads(o_ref, part, head_dim)
        while pending and pending[0][0] <= n:
            gate_stage(*pending.pop(0)[1:])
    for item in pending:
        gate_stage(*item[1:])


def _inproj(x, g, weights, plan, dtypes, gate_weights=()):
    n_tok, d = x.shape
    tm = TOKEN_TILE
    const = lambda i: (0, 0)
    out_specs, out_shape = [], []
    dts = list(dtypes)

    def head_major(nh, head_dim):
        out_specs.append(pl.BlockSpec((nh, tm, head_dim), lambda i: (0, i, 0)))
        out_shape.append(jax.ShapeDtypeStruct((nh, n_tok, head_dim), dts.pop(0)))

    for tr, parts in plan:
        for width, _, head_dim in parts:
            if head_dim == GLA_GATE:
                head_major(HEADS, 2 * DK)
                head_major(HEADS, 2 * DK)
            elif tr:
                out_specs.append(pl.BlockSpec((width, tm), lambda i: (0, i)))
                out_shape.append(jax.ShapeDtypeStruct((width, n_tok), dts.pop(0)))
            elif head_dim is None:
                out_specs.append(pl.BlockSpec((tm, width), lambda i: (i, 0)))
                out_shape.append(jax.ShapeDtypeStruct((n_tok, width), dts.pop(0)))
            else:
                head_major(width // head_dim, head_dim)
    operands = list(weights) + list(gate_weights)
    return pl.pallas_call(
        functools.partial(_inproj_kernel, plan=tuple(plan), n_gate_refs=len(gate_weights)),
        grid=(n_tok // tm,),
        in_specs=[pl.BlockSpec((tm, d), lambda i: (i, 0)),
                  pl.BlockSpec((1, d), const)]
                 + [pl.BlockSpec(w.shape, const) for w in operands],
        out_specs=out_specs,
        out_shape=out_shape,
        compiler_params=_params(1),
        name="inproj",
    )(x, g.reshape(1, d), *operands)


def _scan_lanes(x, op, fill, reverse):
    t = x.shape[-1]
    lane = lax.broadcasted_iota(jnp.int32, x.shape, 1)
    s = 1
    while s < t:
        if reverse:
            x = op(x, jnp.where(lane < t - s, pltpu.roll(x, t - s, axis=1), fill))
        else:
            x = op(x, jnp.where(lane >= s, pltpu.roll(x, s, axis=1), fill))
        s *= 2
    return x


def _mlstm_gates_kernel(g_ref, bias_ref, out_ref):
    n = 2 * HEADS
    lf = _log_sigmoid(g_ref[n:, :] + bias_ref[n:, :])
    fwd = lax.broadcasted_iota(jnp.int32, lf.shape, 0) < HEADS
    b = jnp.where(fwd, _scan_lanes(lf, jnp.add, 0.0, False),
                  _scan_lanes(lf, jnp.add, 0.0, True))
    a = g_ref[:n, :] + bias_ref[:n, :] - b
    amax = jnp.where(fwd, _scan_lanes(a, jnp.maximum, NEG, False),
                     _scan_lanes(a, jnp.maximum, NEG, True))
    out_ref[0, 0] = a * LOG2E
    out_ref[0, 1] = amax * LOG2E
    out_ref[0, 2] = jnp.exp(-(b + amax))


def _mlstm_gates(g, bias, bsz, t):
    return pl.pallas_call(
        _mlstm_gates_kernel,
        grid=(bsz,),
        in_specs=[pl.BlockSpec((4 * HEADS, t), lambda b: (0, b)),
                  pl.BlockSpec((4 * HEADS, 1), lambda b: (0, 0))],
        out_specs=pl.BlockSpec((1, 3, 2 * HEADS, t), lambda b: (b, 0, 0, 0)),
        out_shape=jax.ShapeDtypeStruct((bsz, 3, 2 * HEADS, t), F32),
        compiler_params=_params(1),
        name="mlstm_gates",
    )(g, bias)


def _per_row(x):
    return jnp.broadcast_to(x, (LANES, x.shape[1])).T


def _mlstm_scan_kernel(q_ref, kt_ref, v_ref, row_ref, y_ref,
                       st_ref, cb_ref, ap_ref):
    t = q_ref.shape[1]
    nc = t // CHUNK
    masks = _causal_masks(CHUNK)
    ones = jnp.ones((CHUNK, LANES), BF16)

    st_ref[...] = jnp.zeros_like(st_ref)

    def states(i, a_prevs):
        a_ends = []
        for d, a_prev in enumerate(a_prevs):
            c = i if d == 0 else nc - 1 - i
            s = _chunk_start(c)
            kt = kt_ref[:, pl.ds(s, CHUNK)].astype(F32)
            rows = row_ref[0, 0, :, pl.ds(s, CHUNK)]
            edge = CHUNK - 1 if d == 0 else 0
            a_end = rows[2 + d:3 + d, edge:edge + 1]
            kw = kt * jnp.exp2(rows[d:d + 1, :] - a_end)
            va = jnp.concatenate([v_ref[0, pl.ds(s, CHUNK), :], ones], axis=1)
            state = st_ref[d]
            cb_ref[d, c] = state.astype(BF16)
            ap_ref[d, c] = jnp.broadcast_to(a_prev, (8, LANES))
            st_ref[d] = jnp.exp2(a_prev - a_end) * state + _dot(kw.astype(BF16), va)
            a_ends.append(a_end)
        return tuple(a_ends)

    lax.fori_loop(0, nc, states, (jnp.full((1, 1), NEG, F32),) * 2, unroll=STATE_UNROLL)

    def scores_of(g):
        starts = [_chunk_start(g * GROUP + j) for j in range(GROUP)]
        return [_dot(q_ref[0, pl.ds(s, CHUNK), :], kt_ref[:, pl.ds(s, CHUNK)]) for s in starts]

    def weighted(g, scores):
        lhs = []
        for j, sm in enumerate(scores):
            c = g * GROUP + j
            s = _chunk_start(c)
            rows = row_ref[0, 0, :, pl.ds(s, CHUNK)]
            qf = q_ref[0, pl.ds(s, CHUNK), :].astype(F32)
            for d in (0, 1):
                a_row = rows[d:d + 1, :]
                amax = _per_row(rows[2 + d:3 + d, :])
                a_prev = ap_ref[d, c][0:1, 0:1]
                p = sm * jnp.exp2(jnp.where(masks[d], a_row - amax, NEG))
                sq = qf * jnp.exp2(a_prev - amax)
                lhs.append(jnp.concatenate([p.astype(BF16), sq.astype(BF16)], axis=1))
        return tuple(lhs)

    def readout(g, lhs):
        res = []
        for j in range(GROUP):
            c = g * GROUP + j
            va = jnp.concatenate([v_ref[0, pl.ds(_chunk_start(c), CHUNK), :], ones], axis=1)
            for d in (0, 1):
                res.append(_dot(lhs[2 * j + d], jnp.concatenate([va, cb_ref[d, c]], axis=0)))
        return res

    def finish(g, res):
        for j in range(GROUP):
            s = _chunk_start(g * GROUP + j)
            rows = row_ref[0, 0, :, pl.ds(s, CHUNK)]
            h = None
            for d in (0, 1):
                rd = res[2 * j + d]
                r = 1.0 / jnp.maximum(jnp.abs(rd[:, DV:]), _per_row(rows[4 + d:5 + d, :]))
                hd = rd[:, :DV] * jnp.concatenate([r] * (DV // LANES), axis=1)
                h = hd if h is None else h + hd
            y_ref[0, pl.ds(s, CHUNK), :] = h.astype(y_ref.dtype)

    def step(g, carry):
        finish(g, readout(g, weighted(g, scores_of(g))))
        return carry

    lax.fori_loop(0, nc // GROUP, step, 0)


def _mlstm_scan(q, kt, v, rows, bsz, t):
    assert t % (CHUNK * GROUP) == 0, t
    nc = t // CHUNK
    v_spec = pl.BlockSpec((1, t, DV), lambda b, h: (h, b, 0))
    return pl.pallas_call(
        _mlstm_scan_kernel,
        grid=(bsz, HEADS),
        in_specs=[pl.BlockSpec((1, t, DK), lambda b, h: (h, b, 0)),
                  pl.BlockSpec((DK, t), lambda b, h: (h, b)),
                  v_spec,
                  pl.BlockSpec((1, 1, 6, t), lambda b, h: (b, h, 0, 0))],
        out_specs=v_spec,
        out_shape=jax.ShapeDtypeStruct((HEADS, bsz * t, DV), BF16),
        scratch_shapes=[pltpu.VMEM((2, DK, DVN), F32), pltpu.VMEM((2, nc, DK, DVN), BF16),
                        pltpu.VMEM((2, nc, 8, LANES), F32)],
        compiler_params=_params(2),
        name="mlstm_scan",
    )(q, kt, v, rows)


def _gla_scan_kernel(q_ref, k_ref, v_ref, lgh_ref, lgl_ref, y_ref,
                     bc_ref, st_ref, sb_ref):
    t = q_ref.shape[1]
    nc = t // CHUNK
    sub = GLA_SUB
    masks = _causal_masks(CHUNK)
    tri = tuple(m.astype(BF16) for m in masks)

    gate_rows = GATE_GROUP * CHUNK

    def cumulate(i, carry):
        s = pl.multiple_of(i * gate_rows, gate_rows)
        hi = lgh_ref[0, pl.ds(s, gate_rows), :]
        lo = lgl_ref[0, pl.ds(s, gate_rows), :]
        parts = []
        for j in range(GATE_GROUP):
            rs = slice(j * CHUNK, (j + 1) * CHUNK)
            for d in (0, 1):
                cs = slice(d * DK, (d + 1) * DK)
                parts.append(_dot(tri[d], jnp.concatenate([hi[rs, cs], lo[rs, cs]], axis=1)))
        for j in range(GATE_GROUP):
            for d in (0, 1):
                pd = parts[2 * j + d]
                bc_ref[pl.ds(s + j * CHUNK, CHUNK), d * DK:(d + 1) * DK] = pd[:, :DK] + pd[:, DK:]
        return carry

    lax.fori_loop(0, nc // GATE_GROUP, cumulate, 0)

    st_ref[...] = jnp.zeros_like(st_ref)

    def states(i, carry):
        for d in (0, 1):
            c = i if d == 0 else nc - 1 - i
            s = _chunk_start(c)
            kc = k_ref[0, pl.ds(s, CHUNK), :].astype(F32)
            bc = bc_ref[pl.ds(s, CHUNK), d * DK:(d + 1) * DK]
            edge = CHUNK - 1 if d == 0 else 0
            b_last = bc[edge:edge + 1, :]
            kd = kc * jnp.exp2(b_last - bc)
            decay = jnp.concatenate([_per_row(jnp.exp2(b_last))] * (DV // LANES), axis=1)
            state = st_ref[d]
            sb_ref[d, c] = state.astype(BF16)
            st_ref[d] = decay * state + _dot_tn(kd.astype(BF16), v_ref[0, pl.ds(s, CHUNK), :])
        return carry

    lax.fori_loop(0, nc, states, 0, unroll=STATE_UNROLL)

    zeros = jnp.zeros((sub, DK), BF16)

    def scores_of(g):
        starts = [_chunk_start(g * GROUP + j) for j in range(GROUP)]
        scores, q_state = [], []
        for s in starts:
            qc = q_ref[0, pl.ds(s, CHUNK), :].astype(F32)
            kc = k_ref[0, pl.ds(s, CHUNK), :].astype(F32)
            for d in (0, 1):
                bc = bc_ref[pl.ds(s, CHUNK), d * DK:(d + 1) * DK]
                first, second = ((slice(0, sub), slice(sub, CHUNK)) if d == 0
                                 else (slice(sub, CHUNK), slice(0, sub)))
                ref_row = sub - 1 if d == 0 else sub
                r = bc[ref_row:ref_row + 1, :]
                qe = (qc * jnp.exp2(bc)).astype(BF16)
                qg = (qc[second] * jnp.exp2(bc[second] - r)).astype(BF16)
                kf = (kc[first] * jnp.exp2(-bc[first])).astype(BF16)
                kg = (kc * jnp.exp2(r - bc)).astype(BF16)
                q_first = jnp.concatenate([qe[first], zeros], axis=1)
                q_second = jnp.concatenate([zeros, qg], axis=1)
                if d == 0:
                    lhs = jnp.concatenate([q_first, q_second], axis=0)
                    kf_all = jnp.concatenate([kf, zeros], axis=0)
                else:
                    lhs = jnp.concatenate([q_second, q_first], axis=0)
                    kf_all = jnp.concatenate([zeros, kf], axis=0)
                keys = jnp.concatenate([kf_all, kg], axis=1)
                scores.append(_dot_nt(lhs, keys))
                q_state.append(qe)
        return scores, tuple(q_state)

    def masked(scores):
        return tuple(jnp.where(masks[n % 2], sc, 0.0).astype(BF16) for n, sc in enumerate(scores))

    def readout(g, amats, q_state):
        outs = []
        for j in range(GROUP):
            c = g * GROUP + j
            vc = v_ref[0, pl.ds(_chunk_start(c), CHUNK), :]
            o = None
            for d in (0, 1):
                od = _dot(jnp.concatenate([q_state[2 * j + d], amats[2 * j + d]], axis=1),
                          jnp.concatenate([sb_ref[d, c], vc], axis=0))
                o = od if o is None else o + od
            outs.append(o)
        return outs

    def step(g, carry):
        scores, q_state = scores_of(g)
        for j, o in enumerate(readout(g, masked(scores), q_state)):
            y_ref[0, pl.ds(_chunk_start(g * GROUP + j), CHUNK), :] = o.astype(y_ref.dtype)
        return carry

    lax.fori_loop(0, nc // GROUP, step, 0)


def _gla_scan(q, k, v, lg_hi, lg_lo, bsz, t):
    assert t % (CHUNK * GROUP) == 0 and t % (CHUNK * GATE_GROUP) == 0, t
    nc = t // CHUNK
    qk_spec = pl.BlockSpec((1, t, DK), lambda b, h: (h, b, 0))
    v_spec = pl.BlockSpec((1, t, DV), lambda b, h: (h, b, 0))
    return pl.pallas_call(
        _gla_scan_kernel,
        grid=(bsz, HEADS),
        in_specs=[qk_spec, qk_spec, v_spec, v_spec, v_spec],
        out_specs=v_spec,
        out_shape=jax.ShapeDtypeStruct((HEADS, bsz * t, DV), BF16),
        scratch_shapes=[pltpu.VMEM((t, 2 * DK), F32),
                        pltpu.VMEM((2, DK, DV), F32), pltpu.VMEM((2, nc, DK, DV), BF16)],
        compiler_params=_params(2),
        name="gla_scan",
    )(q, k, v, lg_hi, lg_lo)


def _mix_ffn_kernel(hs_ref, gate_ref, hg_ref, x_ref, wo_ref, gn_ref, wgu_ref, wd_ref, gl_ref, out_ref,
                    *, final, silu_gate):
    heads = []
    for hd in range(hs_ref.shape[0]):
        gh = gate_ref[hd].astype(F32)
        gate = 1.0 + jnp.tanh(gh)
        if silu_gate:
            gate = gh * gate
        heads.append((_rmsnorm(hs_ref[hd].astype(F32), hg_ref[hd]) * gate).astype(BF16))
    x1 = x_ref[...] + _dot(jnp.concatenate(heads, axis=1), wo_ref[...])
    h = _rmsnorm(x1, gn_ref[...]).astype(BF16)
    acc = x1
    d_ff = wd_ref.shape[0]
    for c in range(d_ff // FFN_CHUNK):
        cs = slice(c * FFN_CHUNK, (c + 1) * FFN_CHUNK)
        g = _dot(h, wgu_ref[:, cs])
        u = _dot(h, wgu_ref[:, d_ff + c * FFN_CHUNK:d_ff + (c + 1) * FFN_CHUNK])
        acc = acc + _dot((jax.nn.silu(g) * u).astype(BF16), wd_ref[cs, :])
    if final:
        acc = _rmsnorm(acc, gl_ref[...])
    out_ref[...] = acc


def _mix_ffn(hs, gate, head_g, x, wo, gn, wgu, wd, gl, final, silu_gate):
    n_tok, d = x.shape
    tm = TOKEN_TILE
    const = lambda i: (0, 0)
    resident = lambda w: pl.BlockSpec(w.shape, const, pipeline_mode=pl.Buffered(1))
    tile = pl.BlockSpec((tm, d), lambda i: (i, 0))
    head_tile = pl.BlockSpec((hs.shape[0], tm, hs.shape[2]), lambda i: (0, i, 0))
    vec = pl.BlockSpec((1, d), const)
    return pl.pallas_call(
        functools.partial(_mix_ffn_kernel, final=final, silu_gate=silu_gate),
        grid=(n_tok // tm,),
        in_specs=[head_tile, head_tile, pl.BlockSpec(head_g.shape, lambda i: (0, 0, 0)), tile,
                  resident(wo), vec, resident(wgu), resident(wd), vec],
        out_specs=tile,
        out_shape=jax.ShapeDtypeStruct((n_tok, d), F32),
        compiler_params=_params(1),
        name="mix_ffn",
    )(hs, gate, head_g, x, wo, gn.reshape(1, d), wgu, wd, gl.reshape(1, d))


def _mlstm_mixer(x, bsz, t, g_norm, w, b_gate, head_g):
    qk = HEADS * DK
    vv = HEADS * DV
    plan = ((False, ((qk, DK ** -0.5, DK),)), (True, ((qk, 1.0, None), (4 * HEADS, 1.0, None))),
            (False, ((vv, 1.0, DV),)), (False, ((vv, 0.5, DV),)))
    q, kt, gates, v, o_half = _inproj(x, g_norm, w, plan, (BF16, BF16, F32, BF16, BF16))
    stats = _mlstm_gates(gates, _gate_rows(b_gate).reshape(4 * HEADS, 1), bsz, t)
    rows = stats.reshape(bsz, 3, 2, HEADS, t).transpose(0, 3, 1, 2, 4).reshape(bsz, HEADS, 6, t)
    return _mlstm_scan(q, kt, v, rows, bsz, t), o_half, (0.5 * head_g).reshape(HEADS, 1, DV)


def _gla_mixer(x, bsz, t, g_norm, w, w2, ba, head_g):
    qk = HEADS * DK
    vv = HEADS * DV
    plan = tuple((False, ((n, s, hd),)) for n, s, hd in
                 ((2 * GLA_RANK, 1.0, GLA_GATE), (qk, DK ** -0.5, DK), (qk, 1.0, DK), (vv, 1.0, DV),
                  (vv, 0.5, DV)))
    lg_hi, lg_lo, q, k, v, r_half = _inproj(x, g_norm, w, plan, (BF16,) * 6, gate_weights=(w2, ba))
    return _gla_scan(q, k, v, lg_hi, lg_lo, bsz, t), r_half, head_g.reshape(HEADS, 1, DV)


def _gate_rows(g):
    g4 = g.reshape((4, HEADS) + g.shape[1:])
    return jnp.concatenate([g4[0], g4[2], g4[1], g4[3]], axis=0)


def _split_cols(w, widths):
    out, s = [], 0
    for n in widths:
        out.append(w[:, s:s + n].astype(BF16))
        s += n
    return out


def _gla_gate_weights(w_a2, b_a):
    wf = w_a2[0].reshape(GLA_RANK, HEADS, DK)
    wb = w_a2[1].reshape(GLA_RANK, HEADS, DK)
    z = jnp.zeros_like(wf)
    w2 = jnp.concatenate([jnp.concatenate([wf, z], axis=2),
                          jnp.concatenate([z, wb], axis=2)], axis=0)
    ba = jnp.concatenate([b_a[0].reshape(HEADS, DK), b_a[1].reshape(HEADS, DK)], axis=1)
    return (w2.reshape(2 * GLA_RANK, HEADS * 2 * DK).astype(BF16),
            ba.reshape(1, HEADS * 2 * DK))


def kernel(x_prompt, x_sample, norm_mix_g, norm_ffn_g, norm_final_g, mlstm_w_in, mlstm_b_gate, mlstm_head_g, mlstm_w_out, gla_w_in, gla_w_a2, gla_b_a, gla_head_g, gla_w_out, ffn_w_gu, ffn_w_down):
    depth = norm_mix_g.shape[0]
    d_ff = ffn_w_down.shape[1]
    qk, vv = HEADS * DK, HEADS * DV
    mlstm_w = [_split_cols(w, (qk, qk, vv, vv, 4 * HEADS)) for w in mlstm_w_in]
    mlstm_w = [[wq, jnp.concatenate([wk.T, _gate_rows(wgt.T)], axis=0), wv, wo]
               for wq, wk, wv, wo, wgt in mlstm_w]
    gla_w = [_split_cols(w, (qk, qk, vv, vv, 2 * GLA_RANK)) for w in gla_w_in]
    gla_w = [[wa, wq, wk, wv, wr] for wq, wk, wv, wr, wa in gla_w]
    gla_gate = [_gla_gate_weights(w, b) for w, b in zip(gla_w_a2, gla_b_a)]
    w_out = [mlstm_w_out.astype(BF16), gla_w_out.astype(BF16)]
    w_gu = ffn_w_gu.astype(BF16)
    w_d = ffn_w_down.astype(BF16)

    outs = []
    for x0 in (x_prompt, x_sample):
        bsz, t, d = x0.shape
        x = x0.reshape(bsz * t, d)
        for i in range(depth):
            j = i // 2
            if i % 2 == 0:
                mixed = _mlstm_mixer(x, bsz, t, norm_mix_g[i], mlstm_w[j], mlstm_b_gate[j],
                                     mlstm_head_g[j])
            else:
                mixed = _gla_mixer(x, bsz, t, norm_mix_g[i], gla_w[j], *gla_gate[j], gla_head_g[j])
            x = _mix_ffn(*mixed, x, w_out[i % 2][j], norm_ffn_g[i], w_gu[i], w_d[i], norm_final_g,
                         final=(i == depth - 1), silu_gate=(i % 2 == 1))
        outs.append(x.reshape(bsz, t, d))
    return tuple(outs)
```

```python
import functools

import jax
import jax.numpy as jnp
from jax import lax
from jax.experimental import pallas as pl
from jax.experimental.pallas import tpu as pltpu

F32 = jnp.float32
BF16 = jnp.bfloat16

EPS = 1e-6
LOG2E = 1.4426950408889634
HEADS = 4
DK = 128
DV = 256
LANES = 128
SUBLANES = 8
DVN = DV + LANES
GLA_RANK = 16
GLA_TAU = 16.0
NEG = -1e30

CHUNK = 128
GLA_SUB = 64
STATE_UNROLL = 8
GROUP = 8
GATE_GROUP = 8
TOKEN_TILE = 1024
FFN_CHUNK = 256
VMEM_LIMIT = 56 * 1024 * 1024


def _params(n_parallel):
    return pltpu.CompilerParams(
        dimension_semantics=("parallel",) * n_parallel,
        vmem_limit_bytes=VMEM_LIMIT)


def _rmsnorm(x, g):
    return x * lax.rsqrt(jnp.mean(x * x, axis=-1, keepdims=True) + EPS) * g


def _log_sigmoid(x):
    return jnp.minimum(x, 0.0) - jnp.log(1.0 + jnp.exp(-jnp.abs(x)))


def _dot(a, b):
    return jnp.dot(a, b, preferred_element_type=F32)


def _dot_nt(a, b):
    return lax.dot_general(a, b, (((1,), (1,)), ((), ())), preferred_element_type=F32)


def _dot_tn(a, b):
    return lax.dot_general(a, b, (((0,), (0,)), ((), ())), preferred_element_type=F32)


def _chunk_start(c):
    return c * CHUNK if isinstance(c, int) else pl.multiple_of(c * CHUNK, CHUNK)


def _causal_masks(n):
    ti = lax.broadcasted_iota(jnp.int32, (n, n), 0)
    ji = lax.broadcasted_iota(jnp.int32, (n, n), 1)
    return ji <= ti, ji >= ti


GLA_GATE = "gla_gate"
SIGMOID_GATE = "sigmoid"
SILU_GATE = "silu"
GATE_SLABS = 4


def _write_heads(o_ref, value, head_dim):
    for hd in range(value.shape[1] // head_dim):
        o_ref[hd] = value[:, hd * head_dim:(hd + 1) * head_dim]


def _inproj_kernel(x_ref, g_ref, *refs, plan, n_gate_refs):
    w_refs = refs[:len(plan)]
    gate_refs = list(refs[len(plan):len(plan) + n_gate_refs])
    o_refs = list(refs[len(plan) + n_gate_refs:])
    h = _rmsnorm(x_ref[...], g_ref[...]).astype(BF16)
    pending = []

    def gate_stage(low_rank, w2_ref, ba_ref, hi_ref, lo_ref, rows):
        z = _dot(low_rank[rows].astype(BF16), w2_ref[...]) + ba_ref[...]
        lg = _log_sigmoid(z) * (LOG2E / GLA_TAU)
        hi = lg.astype(BF16)
        lo = (lg - hi.astype(F32)).astype(BF16)
        for hd in range(hi_ref.shape[0]):
            cs = slice(hd * 2 * DK, (hd + 1) * 2 * DK)
            hi_ref[hd, rows, :] = hi[:, cs]
            lo_ref[hd, rows, :] = lo[:, cs]

    for n, (w_ref, (tr, parts)) in enumerate(zip(w_refs, plan)):
        y = _dot_nt(w_ref[...], h) if tr else _dot(h, w_ref[...])
        start = 0
        for width, scale, head_dim in parts:
            part = y[start:start + width, :] if tr else y[:, start:start + width]
            start += width
            if scale == SIGMOID_GATE:
                part = 1.0 + jnp.tanh(0.5 * part)
            elif scale == SILU_GATE:
                half = 0.5 * part
                part = half * (1.0 + jnp.tanh(half))
            elif scale != 1.0:
                part = part * scale
            if head_dim == GLA_GATE:
                stage = (part, gate_refs.pop(0), gate_refs.pop(0), o_refs.pop(0), o_refs.pop(0))
                slab = part.shape[0] // GATE_SLABS
                for r in range(GATE_SLABS):
                    pending.append((n + 1 + r, stage + (slice(r * slab, (r + 1) * slab),)))
                continue
            o_ref = o_refs.pop(0)
            part = part.astype(o_ref.dtype)
            if head_dim is None:
                o_ref[...] = part
            else:
                _write_heads(o_ref, part, head_dim)
        while pending and pending[0][0] <= n:
            gate_stage(*pending.pop(0)[1])
    for _, stage in pending:
        gate_stage(*stage)


def _inproj(x, g, weights, plan, dtypes, gate_weights=()):
    n_tok, d = x.shape
    tm = TOKEN_TILE
    assert n_tok % tm == 0, (n_tok, tm)
    const = lambda i: (0, 0)
    out_specs, out_shape = [], []
    dts = list(dtypes)

    def head_major(nh, head_dim):
        out_specs.append(pl.BlockSpec((nh, tm, head_dim), lambda i: (0, i, 0)))
        out_shape.append(jax.ShapeDtypeStruct((nh, n_tok, head_dim), dts.pop(0)))

    for tr, parts in plan:
        for width, _, head_dim in parts:
            if head_dim == GLA_GATE:
                head_major(HEADS, 2 * DK)
                head_major(HEADS, 2 * DK)
            elif tr:
                out_specs.append(pl.BlockSpec((width, tm), lambda i: (0, i)))
                out_shape.append(jax.ShapeDtypeStruct((width, n_tok), dts.pop(0)))
            elif head_dim is None:
                out_specs.append(pl.BlockSpec((tm, width), lambda i: (i, 0)))
                out_shape.append(jax.ShapeDtypeStruct((n_tok, width), dts.pop(0)))
            else:
                head_major(width // head_dim, head_dim)
    operands = list(weights) + list(gate_weights)
    return pl.pallas_call(
        functools.partial(_inproj_kernel, plan=tuple(plan), n_gate_refs=len(gate_weights)),
        grid=(n_tok // tm,),
        in_specs=[pl.BlockSpec((tm, d), lambda i: (i, 0)),
                  pl.BlockSpec((1, d), const)]
                 + [pl.BlockSpec(w.shape, const) for w in operands],
        out_specs=out_specs,
        out_shape=out_shape,
        compiler_params=_params(1),
        name="inproj",
    )(x, g.reshape(1, d), *operands)


def _scan_lanes(x, op, fill, reverse):
    t = x.shape[-1]
    lane = lax.broadcasted_iota(jnp.int32, x.shape, 1)
    s = 1
    while s < t:
        if reverse:
            x = op(x, jnp.where(lane < t - s, pltpu.roll(x, t - s, axis=1), fill))
        else:
            x = op(x, jnp.where(lane >= s, pltpu.roll(x, s, axis=1), fill))
        s *= 2
    return x


def _mlstm_gates_kernel(g_ref, bias_ref, out_ref):
    n = 2 * HEADS
    lf = _log_sigmoid(g_ref[n:, :] + bias_ref[n:, :])
    fwd = lax.broadcasted_iota(jnp.int32, lf.shape, 0) < HEADS
    b = jnp.where(fwd, _scan_lanes(lf, jnp.add, 0.0, False),
                  _scan_lanes(lf, jnp.add, 0.0, True))
    a = g_ref[:n, :] + bias_ref[:n, :] - b
    amax = jnp.where(fwd, _scan_lanes(a, jnp.maximum, NEG, False),
                     _scan_lanes(a, jnp.maximum, NEG, True))
    out_ref[0, 0] = a * LOG2E
    out_ref[0, 1] = amax * LOG2E
    out_ref[0, 2] = jnp.exp(-(b + amax))


def _mlstm_gates(g, bias, bsz, t):
    return pl.pallas_call(
        _mlstm_gates_kernel,
        grid=(bsz,),
        in_specs=[pl.BlockSpec((4 * HEADS, t), lambda b: (0, b)),
                  pl.BlockSpec((4 * HEADS, 1), lambda b: (0, 0))],
        out_specs=pl.BlockSpec((1, 3, 2 * HEADS, t), lambda b: (b, 0, 0, 0)),
        out_shape=jax.ShapeDtypeStruct((bsz, 3, 2 * HEADS, t), F32),
        compiler_params=_params(1),
        name="mlstm_gates",
    )(g, bias)


def _per_row(x):
    return jnp.broadcast_to(x, (LANES, x.shape[1])).T


def _mlstm_scan_kernel(q_ref, kt_ref, v_ref, row_ref, y_ref,
                       st_ref, cb_ref, ap_ref):
    t = q_ref.shape[1]
    nc = t // CHUNK
    masks = _causal_masks(CHUNK)
    ones = jnp.ones((CHUNK, LANES), BF16)

    st_ref[...] = jnp.zeros_like(st_ref)

    def states(i, a_prevs):
        a_ends = []
        for d, a_prev in enumerate(a_prevs):
            c = i if d == 0 else nc - 1 - i
            s = _chunk_start(c)
            kt = kt_ref[:, pl.ds(s, CHUNK)].astype(F32)
            rows = row_ref[0, 0, :, pl.ds(s, CHUNK)]
            edge = CHUNK - 1 if d == 0 else 0
            a_end = rows[2 + d:3 + d, edge:edge + 1]
            kw = kt * jnp.exp2(rows[d:d + 1, :] - a_end)
            va = jnp.concatenate([v_ref[0, pl.ds(s, CHUNK), :], ones], axis=1)
            state = st_ref[d]
            cb_ref[d, c] = state.astype(BF16)
            ap_ref[d, c] = jnp.broadcast_to(a_prev, (SUBLANES, LANES))
            st_ref[d] = jnp.exp2(a_prev - a_end) * state + _dot(kw.astype(BF16), va)
            a_ends.append(a_end)
        return tuple(a_ends)

    lax.fori_loop(0, nc, states, (jnp.full((1, 1), NEG, F32),) * 2, unroll=STATE_UNROLL)

    def scores_of(g):
        starts = [_chunk_start(g * GROUP + j) for j in range(GROUP)]
        return [_dot(q_ref[0, pl.ds(s, CHUNK), :], kt_ref[:, pl.ds(s, CHUNK)]) for s in starts]

    def weighted(g, scores):
        lhs = []
        for j, sm in enumerate(scores):
            c = g * GROUP + j
            s = _chunk_start(c)
            rows = row_ref[0, 0, :, pl.ds(s, CHUNK)]
            qf = q_ref[0, pl.ds(s, CHUNK), :].astype(F32)
            for d in (0, 1):
                a_row = rows[d:d + 1, :]
                amax = _per_row(rows[2 + d:3 + d, :])
                a_prev = ap_ref[d, c][0:1, 0:1]
                p = sm * jnp.exp2(jnp.where(masks[d], a_row - amax, NEG))
                sq = qf * jnp.exp2(a_prev - amax)
                lhs.append(jnp.concatenate([p.astype(BF16), sq.astype(BF16)], axis=1))
        return tuple(lhs)

    def readout(g, lhs):
        res = []
        for j in range(GROUP):
            c = g * GROUP + j
            va = jnp.concatenate([v_ref[0, pl.ds(_chunk_start(c), CHUNK), :], ones], axis=1)
            for d in (0, 1):
                res.append(_dot(lhs[2 * j + d], jnp.concatenate([va, cb_ref[d, c]], axis=0)))
        return res

    def finish(g, res):
        for j in range(GROUP):
            s = _chunk_start(g * GROUP + j)
            rows = row_ref[0, 0, :, pl.ds(s, CHUNK)]
            h = None
            for d in (0, 1):
                rd = res[2 * j + d]
                r = 1.0 / jnp.maximum(jnp.abs(rd[:, DV:]), _per_row(rows[4 + d:5 + d, :]))
                hd = rd[:, :DV] * jnp.concatenate([r] * (DV // LANES), axis=1)
                h = hd if h is None else h + hd
            y_ref[0, pl.ds(s, CHUNK), :] = h.astype(y_ref.dtype)

    def step(g, carry):
        finish(g, readout(g, weighted(g, scores_of(g))))
        return carry

    lax.fori_loop(0, nc // GROUP, step, 0)


def _mlstm_scan(q, kt, v, rows, bsz, t):
    assert t % (CHUNK * GROUP) == 0, t
    nc = t // CHUNK
    v_spec = pl.BlockSpec((1, t, DV), lambda b, h: (h, b, 0))
    return pl.pallas_call(
        _mlstm_scan_kernel,
        grid=(bsz, HEADS),
        in_specs=[pl.BlockSpec((1, t, DK), lambda b, h: (h, b, 0)),
                  pl.BlockSpec((DK, t), lambda b, h: (h, b)),
                  v_spec,
                  pl.BlockSpec((1, 1, 6, t), lambda b, h: (b, h, 0, 0))],
        out_specs=v_spec,
        out_shape=jax.ShapeDtypeStruct((HEADS, bsz * t, DV), BF16),
        scratch_shapes=[pltpu.VMEM((2, DK, DVN), F32), pltpu.VMEM((2, nc, DK, DVN), BF16),
                        pltpu.VMEM((2, nc, SUBLANES, LANES), F32)],
        compiler_params=_params(2),
        name="mlstm_scan",
    )(q, kt, v, rows)


def _gla_scan_kernel(q_ref, k_ref, v_ref, lgh_ref, lgl_ref, y_ref,
                     bc_ref, st_ref, sb_ref):
    t = q_ref.shape[1]
    nc = t // CHUNK
    sub = GLA_SUB
    masks = _causal_masks(CHUNK)
    tri = tuple(m.astype(BF16) for m in masks)

    gate_rows = GATE_GROUP * CHUNK

    def cumulate(i, carry):
        s = pl.multiple_of(i * gate_rows, gate_rows)
        hi = lgh_ref[0, pl.ds(s, gate_rows), :]
        lo = lgl_ref[0, pl.ds(s, gate_rows), :]
        parts = []
        for j in range(GATE_GROUP):
            rs = slice(j * CHUNK, (j + 1) * CHUNK)
            for d in (0, 1):
                cs = slice(d * DK, (d + 1) * DK)
                parts.append(_dot(tri[d], jnp.concatenate([hi[rs, cs], lo[rs, cs]], axis=1)))
        for j in range(GATE_GROUP):
            for d in (0, 1):
                pd = parts[2 * j + d]
                bc_ref[pl.ds(s + j * CHUNK, CHUNK), d * DK:(d + 1) * DK] = pd[:, :DK] + pd[:, DK:]
        return carry

    lax.fori_loop(0, nc // GATE_GROUP, cumulate, 0)

    st_ref[...] = jnp.zeros_like(st_ref)

    def states(i, carry):
        for d in (0, 1):
            c = i if d == 0 else nc - 1 - i
            s = _chunk_start(c)
            kc = k_ref[0, pl.ds(s, CHUNK), :].astype(F32)
            bc = bc_ref[pl.ds(s, CHUNK), d * DK:(d + 1) * DK]
            edge = CHUNK - 1 if d == 0 else 0
            b_last = bc[edge:edge + 1, :]
            kd = kc * jnp.exp2(b_last - bc)
            decay = jnp.concatenate([_per_row(jnp.exp2(b_last))] * (DV // LANES), axis=1)
            state = st_ref[d]
            sb_ref[d, c] = state.astype(BF16)
            st_ref[d] = decay * state + _dot_tn(kd.astype(BF16), v_ref[0, pl.ds(s, CHUNK), :])
        return carry

    lax.fori_loop(0, nc, states, 0, unroll=STATE_UNROLL)

    zeros = jnp.zeros((sub, DK), BF16)

    def scores_of(g):
        starts = [_chunk_start(g * GROUP + j) for j in range(GROUP)]
        scores, q_state = [], []
        for s in starts:
            qc = q_ref[0, pl.ds(s, CHUNK), :].astype(F32)
            kc = k_ref[0, pl.ds(s, CHUNK), :].astype(F32)
            for d in (0, 1):
                bc = bc_ref[pl.ds(s, CHUNK), d * DK:(d + 1) * DK]
                first, second = ((slice(0, sub), slice(sub, CHUNK)) if d == 0
                                 else (slice(sub, CHUNK), slice(0, sub)))
                ref_row = sub - 1 if d == 0 else sub
                r = bc[ref_row:ref_row + 1, :]
                qe = (qc * jnp.exp2(bc)).astype(BF16)
                qg = (qc[second] * jnp.exp2(bc[second] - r)).astype(BF16)
                kf = (kc[first] * jnp.exp2(-bc[first])).astype(BF16)
                kg = (kc * jnp.exp2(r - bc)).astype(BF16)
                q_first = jnp.concatenate([qe[first], zeros], axis=1)
                q_second = jnp.concatenate([zeros, qg], axis=1)
                if d == 0:
                    lhs = jnp.concatenate([q_first, q_second], axis=0)
                    kf_all = jnp.concatenate([kf, zeros], axis=0)
                else:
                    lhs = jnp.concatenate([q_second, q_first], axis=0)
                    kf_all = jnp.concatenate([zeros, kf], axis=0)
                keys = jnp.concatenate([kf_all, kg], axis=1)
                scores.append(_dot_nt(lhs, keys))
                q_state.append(qe)
        return scores, tuple(q_state)

    def masked(scores):
        return tuple(jnp.where(masks[n % 2], sc, 0.0).astype(BF16) for n, sc in enumerate(scores))

    def readout(g, amats, q_state):
        outs = []
        for j in range(GROUP):
            c = g * GROUP + j
            vc = v_ref[0, pl.ds(_chunk_start(c), CHUNK), :]
            o = None
            for d in (0, 1):
                od = _dot(jnp.concatenate([q_state[2 * j + d], amats[2 * j + d]], axis=1),
                          jnp.concatenate([sb_ref[d, c], vc], axis=0))
                o = od if o is None else o + od
            outs.append(o)
        return outs

    def step(g, carry):
        scores, q_state = scores_of(g)
        for j, o in enumerate(readout(g, masked(scores), q_state)):
            y_ref[0, pl.ds(_chunk_start(g * GROUP + j), CHUNK), :] = o.astype(y_ref.dtype)
        return carry

    lax.fori_loop(0, nc // GROUP, step, 0)


def _gla_scan(q, k, v, lg_hi, lg_lo, bsz, t):
    assert t % (CHUNK * GROUP) == 0 and t % (CHUNK * GATE_GROUP) == 0, t
    nc = t // CHUNK
    qk_spec = pl.BlockSpec((1, t, DK), lambda b, h: (h, b, 0))
    v_spec = pl.BlockSpec((1, t, DV), lambda b, h: (h, b, 0))
    return pl.pallas_call(
        _gla_scan_kernel,
        grid=(bsz, HEADS),
        in_specs=[qk_spec, qk_spec, v_spec, v_spec, v_spec],
        out_specs=v_spec,
        out_shape=jax.ShapeDtypeStruct((HEADS, bsz * t, DV), BF16),
        scratch_shapes=[pltpu.VMEM((t, 2 * DK), F32),
                        pltpu.VMEM((2, DK, DV), F32), pltpu.VMEM((2, nc, DK, DV), BF16)],
        compiler_params=_params(2),
        name="gla_scan",
    )(q, k, v, lg_hi, lg_lo)


def _mix_ffn_kernel(hs_ref, gate_ref, hg_ref, x_ref, wo_ref, gn_ref, wgu_ref, wd_ref, gl_ref, out_ref,
                    *, final):
    heads = [(_rmsnorm(hs_ref[hd].astype(F32), hg_ref[hd]) * gate_ref[hd].astype(F32)).astype(BF16)
             for hd in range(hs_ref.shape[0])]
    x1 = x_ref[...] + _dot(jnp.concatenate(heads, axis=1), wo_ref[...])
    h = _rmsnorm(x1, gn_ref[...]).astype(BF16)
    acc = x1
    d_ff = wd_ref.shape[0]
    for c in range(d_ff // FFN_CHUNK):
        cs = slice(c * FFN_CHUNK, (c + 1) * FFN_CHUNK)
        g = _dot(h, wgu_ref[:, cs])
        u = _dot(h, wgu_ref[:, d_ff + c * FFN_CHUNK:d_ff + (c + 1) * FFN_CHUNK])
        acc = acc + _dot((jax.nn.silu(g) * u).astype(BF16), wd_ref[cs, :])
    if final:
        acc = _rmsnorm(acc, gl_ref[...])
    out_ref[...] = acc


def _mix_ffn(hs, gate, head_g, x, wo, gn, wgu, wd, gl, final):
    n_tok, d = x.shape
    tm = TOKEN_TILE
    assert n_tok % tm == 0 and wd.shape[0] % FFN_CHUNK == 0, (n_tok, wd.shape)
    const = lambda i: (0, 0)
    resident = lambda w: pl.BlockSpec(w.shape, const, pipeline_mode=pl.Buffered(1))
    tile = pl.BlockSpec((tm, d), lambda i: (i, 0))
    head_tile = pl.BlockSpec((hs.shape[0], tm, hs.shape[2]), lambda i: (0, i, 0))
    vec = pl.BlockSpec((1, d), const)
    return pl.pallas_call(
        functools.partial(_mix_ffn_kernel, final=final),
        grid=(n_tok // tm,),
        in_specs=[head_tile, head_tile, pl.BlockSpec(head_g.shape, lambda i: (0, 0, 0)), tile,
                  resident(wo), vec, resident(wgu), resident(wd), vec],
        out_specs=tile,
        out_shape=jax.ShapeDtypeStruct((n_tok, d), F32),
        compiler_params=_params(1),
        name="mix_ffn",
    )(hs, gate, head_g, x, wo, gn.reshape(1, d), wgu, wd, gl.reshape(1, d))


def _mlstm_mixer(x, bsz, t, g_norm, w, b_gate, head_g):
    qk = HEADS * DK
    vv = HEADS * DV
    plan = ((False, ((qk, DK ** -0.5, DK),)), (True, ((qk, 1.0, None), (4 * HEADS, 1.0, None))),
            (False, ((vv, 1.0, DV),)), (False, ((vv, SIGMOID_GATE, DV),)))
    q, kt, gates, v, gate = _inproj(x, g_norm, w, plan, (BF16, BF16, F32, BF16, BF16))
    stats = _mlstm_gates(gates, _gate_rows(b_gate).reshape(4 * HEADS, 1), bsz, t)
    rows = stats.reshape(bsz, 3, 2, HEADS, t).transpose(0, 3, 1, 2, 4).reshape(bsz, HEADS, 6, t)
    return _mlstm_scan(q, kt, v, rows, bsz, t), gate, (0.5 * head_g).reshape(HEADS, 1, DV)


def _gla_mixer(x, bsz, t, g_norm, w, w2, ba, head_g):
    qk = HEADS * DK
    vv = HEADS * DV
    plan = tuple((False, ((n, s, hd),)) for n, s, hd in
                 ((2 * GLA_RANK, 1.0, GLA_GATE), (qk, DK ** -0.5, DK), (qk, 1.0, DK), (vv, 1.0, DV),
                  (vv, SILU_GATE, DV)))
    lg_hi, lg_lo, q, k, v, gate = _inproj(x, g_norm, w, plan, (BF16,) * 6, gate_weights=(w2, ba))
    return _gla_scan(q, k, v, lg_hi, lg_lo, bsz, t), gate, head_g.reshape(HEADS, 1, DV)


def _gate_rows(g):
    g4 = g.reshape((4, HEADS) + g.shape[1:])
    return jnp.concatenate([g4[0], g4[2], g4[1], g4[3]], axis=0)


def _split_cols(w, widths):
    out, s = [], 0
    for n in widths:
        out.append(w[:, s:s + n].astype(BF16))
        s += n
    return out


def _gla_gate_weights(w_a2, b_a):
    wf = w_a2[0].reshape(GLA_RANK, HEADS, DK)
    wb = w_a2[1].reshape(GLA_RANK, HEADS, DK)
    z = jnp.zeros_like(wf)
    w2 = jnp.concatenate([jnp.concatenate([wf, z], axis=2),
                          jnp.concatenate([z, wb], axis=2)], axis=0)
    ba = jnp.concatenate([b_a[0].reshape(HEADS, DK), b_a[1].reshape(HEADS, DK)], axis=1)
    return (w2.reshape(2 * GLA_RANK, HEADS * 2 * DK).astype(BF16),
            ba.reshape(1, HEADS * 2 * DK))


def kernel(x_prompt, x_sample, norm_mix_g, norm_ffn_g, norm_final_g, mlstm_w_in, mlstm_b_gate, mlstm_head_g, mlstm_w_out, gla_w_in, gla_w_a2, gla_b_a, gla_head_g, gla_w_out, ffn_w_gu, ffn_w_down):
    depth = norm_mix_g.shape[0]
    d_ff = ffn_w_down.shape[1]
    qk, vv = HEADS * DK, HEADS * DV
    mlstm_w = [_split_cols(w, (qk, qk, vv, vv, 4 * HEADS)) for w in mlstm_w_in]
    mlstm_w = [[wq, jnp.concatenate([wk.T, _gate_rows(wgt.T)], axis=0), wv, wo]
               for wq, wk, wv, wo, wgt in mlstm_w]
    gla_w = [_split_cols(w, (qk, qk, vv, vv, 2 * GLA_RANK)) for w in gla_w_in]
    gla_w = [[wa, wq, wk, wv, wr] for wq, wk, wv, wr, wa in gla_w]
    gla_gate = [_gla_gate_weights(w, b) for w, b in zip(gla_w_a2, gla_b_a)]
    w_out = [mlstm_w_out.astype(BF16), gla_w_out.astype(BF16)]
    w_gu = ffn_w_gu.astype(BF16)
    w_d = ffn_w_down.astype(BF16)

    outs = []
    for x0 in (x_prompt, x_sample):
        bsz, t, d = x0.shape
        x = x0.reshape(bsz * t, d)
        for i in range(depth):
            j = i // 2
            if i % 2 == 0:
                mixed = _mlstm_mixer(x, bsz, t, norm_mix_g[i], mlstm_w[j], mlstm_b_gate[j],
                                     mlstm_head_g[j])
            else:
                mixed = _gla_mixer(x, bsz, t, norm_mix_g[i], gla_w[j], *gla_gate[j], gla_head_g[j])
            x = _mix_ffn(*mixed, x, w_out[i % 2][j], norm_ffn_g[i], w_gu[i], w_d[i], norm_final_g,
                         final=(i == depth - 1))
        outs.append(x.reshape(bsz, t, d))
    return tuple(outs)
```

```python
import functools

import jax
import jax.numpy as jnp
from jax import lax
from jax.experimental import pallas as pl
from jax.experimental.pallas import tpu as pltpu

F32 = jnp.float32
BF16 = jnp.bfloat16

EPS = 1e-6
LOG2E = 1.4426950408889634
HEADS = 4
DK = 128
DV = 256
LANES = 128
SUBLANES = 8
DVN = DV + LANES
GLA_RANK = 16
GLA_TAU = 16.0
NEG = -1e30

CHUNK = 128
GLA_SUB = 64
STATE_UNROLL = 16
GROUP = 16
GATE_GROUP = 8
TOKEN_TILE = 1024
FFN_CHUNK = 256
VMEM_LIMIT = 56 * 1024 * 1024


def _params(n_parallel):
    return pltpu.CompilerParams(
        dimension_semantics=("parallel",) * n_parallel,
        vmem_limit_bytes=VMEM_LIMIT)


def _rmsnorm(x, g):
    return x * lax.rsqrt(jnp.mean(x * x, axis=-1, keepdims=True) + EPS) * g


def _log_sigmoid(x):
    return jnp.minimum(x, 0.0) - jnp.log(1.0 + jnp.exp(-jnp.abs(x)))


def _dot(a, b):
    return jnp.dot(a, b, preferred_element_type=F32)


def _dot_nt(a, b):
    return lax.dot_general(a, b, (((1,), (1,)), ((), ())), preferred_element_type=F32)


def _dot_tn(a, b):
    return lax.dot_general(a, b, (((0,), (0,)), ((), ())), preferred_element_type=F32)


def _chunk_start(c):
    return c * CHUNK if isinstance(c, int) else pl.multiple_of(c * CHUNK, CHUNK)


def _causal_masks(n):
    ti = lax.broadcasted_iota(jnp.int32, (n, n), 0)
    ji = lax.broadcasted_iota(jnp.int32, (n, n), 1)
    return ji <= ti, ji >= ti


GLA_GATE = "gla_gate"
SIGMOID_GATE = "sigmoid"
SILU_GATE = "silu"
GATE_SLABS = 4


def _write_heads(o_ref, value, head_dim):
    for hd in range(value.shape[1] // head_dim):
        o_ref[hd] = value[:, hd * head_dim:(hd + 1) * head_dim]


def _inproj_kernel(x_ref, g_ref, *refs, plan, n_gate_refs):
    w_refs = refs[:len(plan)]
    gate_refs = list(refs[len(plan):len(plan) + n_gate_refs])
    o_refs = list(refs[len(plan) + n_gate_refs:])
    h = _rmsnorm(x_ref[...], g_ref[...]).astype(BF16)
    pending = []

    def gate_stage(low_rank, w2_ref, ba_ref, hi_ref, lo_ref, rows):
        z = _dot(low_rank[rows].astype(BF16), w2_ref[...]) + ba_ref[...]
        lg = _log_sigmoid(z) * (LOG2E / GLA_TAU)
        hi = lg.astype(BF16)
        lo = (lg - hi.astype(F32)).astype(BF16)
        for hd in range(hi_ref.shape[0]):
            cs = slice(hd * 2 * DK, (hd + 1) * 2 * DK)
            hi_ref[hd, rows, :] = hi[:, cs]
            lo_ref[hd, rows, :] = lo[:, cs]

    for n, (w_ref, (tr, parts)) in enumerate(zip(w_refs, plan)):
        y = _dot_nt(w_ref[...], h) if tr else _dot(h, w_ref[...])
        start = 0
        for width, scale, head_dim in parts:
            part = y[start:start + width, :] if tr else y[:, start:start + width]
            start += width
            if scale == SIGMOID_GATE:
                part = 1.0 + jnp.tanh(0.5 * part)
            elif scale == SILU_GATE:
                half = 0.5 * part
                part = half * (1.0 + jnp.tanh(half))
            elif scale != 1.0:
                part = part * scale
            if head_dim == GLA_GATE:
                stage = (part, gate_refs.pop(0), gate_refs.pop(0), o_refs.pop(0), o_refs.pop(0))
                slab = part.shape[0] // GATE_SLABS
                for r in range(GATE_SLABS):
                    pending.append((n + 1 + r, stage + (slice(r * slab, (r + 1) * slab),)))
                continue
            o_ref = o_refs.pop(0)
            part = part.astype(o_ref.dtype)
            if head_dim is None:
                o_ref[...] = part
            else:
                _write_heads(o_ref, part, head_dim)
        while pending and pending[0][0] <= n:
            gate_stage(*pending.pop(0)[1])
    for _, stage in pending:
        gate_stage(*stage)


def _inproj(x, g, weights, plan, dtypes, gate_weights=()):
    n_tok, d = x.shape
    tm = TOKEN_TILE
    assert n_tok % tm == 0, (n_tok, tm)
    const = lambda i: (0, 0)
    out_specs, out_shape = [], []
    dts = list(dtypes)

    def head_major(nh, head_dim):
        out_specs.append(pl.BlockSpec((nh, tm, head_dim), lambda i: (0, i, 0)))
        out_shape.append(jax.ShapeDtypeStruct((nh, n_tok, head_dim), dts.pop(0)))

    for tr, parts in plan:
        for width, _, head_dim in parts:
            if head_dim == GLA_GATE:
                head_major(HEADS, 2 * DK)
                head_major(HEADS, 2 * DK)
            elif tr:
                out_specs.append(pl.BlockSpec((width, tm), lambda i: (0, i)))
                out_shape.append(jax.ShapeDtypeStruct((width, n_tok), dts.pop(0)))
            elif head_dim is None:
                out_specs.append(pl.BlockSpec((tm, width), lambda i: (i, 0)))
                out_shape.append(jax.ShapeDtypeStruct((n_tok, width), dts.pop(0)))
            else:
                head_major(width // head_dim, head_dim)
    operands = list(weights) + list(gate_weights)
    return pl.pallas_call(
        functools.partial(_inproj_kernel, plan=tuple(plan), n_gate_refs=len(gate_weights)),
        grid=(n_tok // tm,),
        in_specs=[pl.BlockSpec((tm, d), lambda i: (i, 0)),
                  pl.BlockSpec((1, d), const)]
                 + [pl.BlockSpec(w.shape, const) for w in operands],
        out_specs=out_specs,
        out_shape=out_shape,
        compiler_params=_params(1),
        name="inproj",
    )(x, g.reshape(1, d), *operands)


def _scan_lanes(x, op, fill, reverse):
    t = x.shape[-1]
    lane = lax.broadcasted_iota(jnp.int32, x.shape, 1)
    s = 1
    while s < t:
        if reverse:
            x = op(x, jnp.where(lane < t - s, pltpu.roll(x, t - s, axis=1), fill))
        else:
            x = op(x, jnp.where(lane >= s, pltpu.roll(x, s, axis=1), fill))
        s *= 2
    return x


def _mlstm_gates_kernel(g_ref, bias_ref, out_ref):
    n = 2 * HEADS
    lf = _log_sigmoid(g_ref[n:, :] + bias_ref[n:, :])
    fwd = lax.broadcasted_iota(jnp.int32, lf.shape, 0) < HEADS
    b = jnp.where(fwd, _scan_lanes(lf, jnp.add, 0.0, False),
                  _scan_lanes(lf, jnp.add, 0.0, True))
    a = g_ref[:n, :] + bias_ref[:n, :] - b
    amax = jnp.where(fwd, _scan_lanes(a, jnp.maximum, NEG, False),
                     _scan_lanes(a, jnp.maximum, NEG, True))
    out_ref[0, 0] = a * LOG2E
    out_ref[0, 1] = amax * LOG2E
    out_ref[0, 2] = jnp.exp(-(b + amax))


def _mlstm_gates(g, bias, bsz, t):
    return pl.pallas_call(
        _mlstm_gates_kernel,
        grid=(bsz,),
        in_specs=[pl.BlockSpec((4 * HEADS, t), lambda b: (0, b)),
                  pl.BlockSpec((4 * HEADS, 1), lambda b: (0, 0))],
        out_specs=pl.BlockSpec((1, 3, 2 * HEADS, t), lambda b: (b, 0, 0, 0)),
        out_shape=jax.ShapeDtypeStruct((bsz, 3, 2 * HEADS, t), F32),
        compiler_params=_params(1),
        name="mlstm_gates",
    )(g, bias)


def _per_row(x):
    return jnp.broadcast_to(x, (LANES, x.shape[1])).T


def _mlstm_scan_kernel(q_ref, kt_ref, v_ref, row_ref, y_ref,
                       st_ref, cb_ref, ap_ref):
    t = q_ref.shape[1]
    nc = t // CHUNK
    masks = _causal_masks(CHUNK)
    ones = jnp.ones((CHUNK, LANES), BF16)

    st_ref[...] = jnp.zeros_like(st_ref)

    def states(i, a_prevs):
        a_ends = []
        for d, a_prev in enumerate(a_prevs):
            c = i if d == 0 else nc - 1 - i
            s = _chunk_start(c)
            kt = kt_ref[:, pl.ds(s, CHUNK)].astype(F32)
            rows = row_ref[0, 0, :, pl.ds(s, CHUNK)]
            edge = CHUNK - 1 if d == 0 else 0
            a_end = rows[2 + d:3 + d, edge:edge + 1]
            kw = kt * jnp.exp2(rows[d:d + 1, :] - a_end)
            va = jnp.concatenate([v_ref[0, pl.ds(s, CHUNK), :], ones], axis=1)
            state = st_ref[d]
            cb_ref[d, c] = state.astype(BF16)
            ap_ref[d, c] = jnp.broadcast_to(a_prev, (SUBLANES, LANES))
            st_ref[d] = jnp.exp2(a_prev - a_end) * state + _dot(kw.astype(BF16), va)
            a_ends.append(a_end)
        return tuple(a_ends)

    lax.fori_loop(0, nc, states, (jnp.full((1, 1), NEG, F32),) * 2, unroll=STATE_UNROLL)

    def scores_of(g):
        starts = [_chunk_start(g * GROUP + j) for j in range(GROUP)]
        return [_dot(q_ref[0, pl.ds(s, CHUNK), :], kt_ref[:, pl.ds(s, CHUNK)]) for s in starts]

    def weighted(g, scores):
        lhs = []
        for j, sm in enumerate(scores):
            c = g * GROUP + j
            s = _chunk_start(c)
            rows = row_ref[0, 0, :, pl.ds(s, CHUNK)]
            qf = q_ref[0, pl.ds(s, CHUNK), :].astype(F32)
            for d in (0, 1):
                a_row = rows[d:d + 1, :]
                amax = _per_row(rows[2 + d:3 + d, :])
                a_prev = ap_ref[d, c][0:1, 0:1]
                p = sm * jnp.exp2(jnp.where(masks[d], a_row - amax, NEG))
                sq = qf * jnp.exp2(a_prev - amax)
                lhs.append(jnp.concatenate([p.astype(BF16), sq.astype(BF16)], axis=1))
        return tuple(lhs)

    def readout(g, lhs):
        res = []
        for j in range(GROUP):
            c = g * GROUP + j
            va = jnp.concatenate([v_ref[0, pl.ds(_chunk_start(c), CHUNK), :], ones], axis=1)
            for d in (0, 1):
                res.append(_dot(lhs[2 * j + d], jnp.concatenate([va, cb_ref[d, c]], axis=0)))
        return res

    def finish(g, res):
        for j in range(GROUP):
            s = _chunk_start(g * GROUP + j)
            rows = row_ref[0, 0, :, pl.ds(s, CHUNK)]
            h = None
            for d in (0, 1):
                rd = res[2 * j + d]
                r = 1.0 / jnp.maximum(jnp.abs(rd[:, DV:]), _per_row(rows[4 + d:5 + d, :]))
                hd = rd[:, :DV] * jnp.concatenate([r] * (DV // LANES), axis=1)
                h = hd if h is None else h + hd
            y_ref[0, pl.ds(s, CHUNK), :] = h.astype(y_ref.dtype)

    def step(g, carry):
        finish(g, readout(g, weighted(g, scores_of(g))))
        return carry

    lax.fori_loop(0, nc // GROUP, step, 0)


def _mlstm_scan(q, kt, v, rows, bsz, t):
    assert t % (CHUNK * GROUP) == 0, t
    nc = t // CHUNK
    v_spec = pl.BlockSpec((1, t, DV), lambda b, h: (h, b, 0))
    return pl.pallas_call(
        _mlstm_scan_kernel,
        grid=(bsz, HEADS),
        in_specs=[pl.BlockSpec((1, t, DK), lambda b, h: (h, b, 0)),
                  pl.BlockSpec((DK, t), lambda b, h: (h, b)),
                  v_spec,
                  pl.BlockSpec((1, 1, 6, t), lambda b, h: (b, h, 0, 0))],
        out_specs=v_spec,
        out_shape=jax.ShapeDtypeStruct((HEADS, bsz * t, DV), BF16),
        scratch_shapes=[pltpu.VMEM((2, DK, DVN), F32), pltpu.VMEM((2, nc, DK, DVN), BF16),
                        pltpu.VMEM((2, nc, SUBLANES, LANES), F32)],
        compiler_params=_params(2),
        name="mlstm_scan",
    )(q, kt, v, rows)


def _gla_scan_kernel(q_ref, k_ref, v_ref, lgh_ref, lgl_ref, y_ref,
                     bc_ref, st_ref, sb_ref):
    t = q_ref.shape[1]
    nc = t // CHUNK
    sub = GLA_SUB
    masks = _causal_masks(CHUNK)
    tri = tuple(m.astype(BF16) for m in masks)

    gate_rows = GATE_GROUP * CHUNK

    def cumulate(i, carry):
        s = pl.multiple_of(i * gate_rows, gate_rows)
        hi = lgh_ref[0, pl.ds(s, gate_rows), :]
        lo = lgl_ref[0, pl.ds(s, gate_rows), :]
        parts = []
        for j in range(GATE_GROUP):
            rs = slice(j * CHUNK, (j + 1) * CHUNK)
            for d in (0, 1):
                cs = slice(d * DK, (d + 1) * DK)
                parts.append(_dot(tri[d], jnp.concatenate([hi[rs, cs], lo[rs, cs]], axis=1)))
        for j in range(GATE_GROUP):
            for d in (0, 1):
                pd = parts[2 * j + d]
                bc_ref[pl.ds(s + j * CHUNK, CHUNK), d * DK:(d + 1) * DK] = pd[:, :DK] + pd[:, DK:]
        return carry

    lax.fori_loop(0, nc // GATE_GROUP, cumulate, 0)

    st_ref[...] = jnp.zeros_like(st_ref)

    def states(i, carry):
        for d in (0, 1):
            c = i if d == 0 else nc - 1 - i
            s = _chunk_start(c)
            kc = k_ref[0, pl.ds(s, CHUNK), :].astype(F32)
            bc = bc_ref[pl.ds(s, CHUNK), d * DK:(d + 1) * DK]
            edge = CHUNK - 1 if d == 0 else 0
            b_last = bc[edge:edge + 1, :]
            kd = kc * jnp.exp2(b_last - bc)
            decay = jnp.concatenate([_per_row(jnp.exp2(b_last))] * (DV // LANES), axis=1)
            state = st_ref[d]
            sb_ref[d, c] = state.astype(BF16)
            st_ref[d] = decay * state + _dot_tn(kd.astype(BF16), v_ref[0, pl.ds(s, CHUNK), :])
        return carry

    lax.fori_loop(0, nc, states, 0, unroll=STATE_UNROLL)

    zeros = jnp.zeros((sub, DK), BF16)

    def scores_of(g):
        starts = [_chunk_start(g * GROUP + j) for j in range(GROUP)]
        scores, q_state = [], []
        for s in starts:
            qc = q_ref[0, pl.ds(s, CHUNK), :].astype(F32)
            kc = k_ref[0, pl.ds(s, CHUNK), :].astype(F32)
            for d in (0, 1):
                bc = bc_ref[pl.ds(s, CHUNK), d * DK:(d + 1) * DK]
                first, second = ((slice(0, sub), slice(sub, CHUNK)) if d == 0
                                 else (slice(sub, CHUNK), slice(0, sub)))
                ref_row = sub - 1 if d == 0 else sub
                r = bc[ref_row:ref_row + 1, :]
                qe = (qc * jnp.exp2(bc)).astype(BF16)
                qg = (qc[second] * jnp.exp2(bc[second] - r)).astype(BF16)
                kf = (kc[first] * jnp.exp2(-bc[first])).astype(BF16)
                kg = (kc * jnp.exp2(r - bc)).astype(BF16)
                q_first = jnp.concatenate([qe[first], zeros], axis=1)
                q_second = jnp.concatenate([zeros, qg], axis=1)
                if d == 0:
                    lhs = jnp.concatenate([q_first, q_second], axis=0)
                    kf_all = jnp.concatenate([kf, zeros], axis=0)
                else:
                    lhs = jnp.concatenate([q_second, q_first], axis=0)
                    kf_all = jnp.concatenate([zeros, kf], axis=0)
                keys = jnp.concatenate([kf_all, kg], axis=1)
                scores.append(_dot_nt(lhs, keys))
                q_state.append(qe)
        return scores, tuple(q_state)

    def masked(scores):
        return tuple(jnp.where(masks[n % 2], sc, 0.0).astype(BF16) for n, sc in enumerate(scores))

    def readout(g, amats, q_state):
        outs = []
        for j in range(GROUP):
            c = g * GROUP + j
            vc = v_ref[0, pl.ds(_chunk_start(c), CHUNK), :]
            o = None
            for d in (0, 1):
                od = _dot(jnp.concatenate([q_state[2 * j + d], amats[2 * j + d]], axis=1),
                          jnp.concatenate([sb_ref[d, c], vc], axis=0))
                o = od if o is None else o + od
            outs.append(o)
        return outs

    def step(g, carry):
        scores, q_state = scores_of(g)
        for j, o in enumerate(readout(g, masked(scores), q_state)):
            y_ref[0, pl.ds(_chunk_start(g * GROUP + j), CHUNK), :] = o.astype(y_ref.dtype)
        return carry

    lax.fori_loop(0, nc // GROUP, step, 0)


def _gla_scan(q, k, v, lg_hi, lg_lo, bsz, t):
    assert t % (CHUNK * GROUP) == 0 and t % (CHUNK * GATE_GROUP) == 0, t
    nc = t // CHUNK
    qk_spec = pl.BlockSpec((1, t, DK), lambda b, h: (h, b, 0))
    v_spec = pl.BlockSpec((1, t, DV), lambda b, h: (h, b, 0))
    return pl.pallas_call(
        _gla_scan_kernel,
        grid=(bsz, HEADS),
        in_specs=[qk_spec, qk_spec, v_spec, v_spec, v_spec],
        out_specs=v_spec,
        out_shape=jax.ShapeDtypeStruct((HEADS, bsz * t, DV), BF16),
        scratch_shapes=[pltpu.VMEM((t, 2 * DK), F32),
                        pltpu.VMEM((2, DK, DV), F32), pltpu.VMEM((2, nc, DK, DV), BF16)],
        compiler_params=_params(2),
        name="gla_scan",
    )(q, k, v, lg_hi, lg_lo)


def _mix_ffn_kernel(hs_ref, gate_ref, hg_ref, x_ref, wo_ref, gn_ref, wgu_ref, wd_ref, gl_ref, out_ref,
                    *, final):
    heads = [(_rmsnorm(hs_ref[hd].astype(F32), hg_ref[hd]) * gate_ref[hd].astype(F32)).astype(BF16)
             for hd in range(hs_ref.shape[0])]
    x1 = x_ref[...] + _dot(jnp.concatenate(heads, axis=1), wo_ref[...])
    h = _rmsnorm(x1, gn_ref[...]).astype(BF16)
    acc = x1
    d_ff = wd_ref.shape[0]
    for c in range(d_ff // FFN_CHUNK):
        cs = slice(c * FFN_CHUNK, (c + 1) * FFN_CHUNK)
        g = _dot(h, wgu_ref[:, cs])
        u = _dot(h, wgu_ref[:, d_ff + c * FFN_CHUNK:d_ff + (c + 1) * FFN_CHUNK])
        acc = acc + _dot((jax.nn.silu(g) * u).astype(BF16), wd_ref[cs, :])
    if final:
        acc = _rmsnorm(acc, gl_ref[...])
    out_ref[...] = acc


def _mix_ffn(hs, gate, head_g, x, wo, gn, wgu, wd, gl, final):
    n_tok, d = x.shape
    tm = TOKEN_TILE
    assert n_tok % tm == 0 and wd.shape[0] % FFN_CHUNK == 0, (n_tok, wd.shape)
    const = lambda i: (0, 0)
    resident = lambda w: pl.BlockSpec(w.shape, const, pipeline_mode=pl.Buffered(1))
    tile = pl.BlockSpec((tm, d), lambda i: (i, 0))
    head_tile = pl.BlockSpec((hs.shape[0], tm, hs.shape[2]), lambda i: (0, i, 0))
    vec = pl.BlockSpec((1, d), const)
    return pl.pallas_call(
        functools.partial(_mix_ffn_kernel, final=final),
        grid=(n_tok // tm,),
        in_specs=[head_tile, head_tile, pl.BlockSpec(head_g.shape, lambda i: (0, 0, 0)), tile,
                  resident(wo), vec, resident(wgu), resident(wd), vec],
        out_specs=tile,
        out_shape=jax.ShapeDtypeStruct((n_tok, d), F32),
        compiler_params=_params(1),
        name="mix_ffn",
    )(hs, gate, head_g, x, wo, gn.reshape(1, d), wgu, wd, gl.reshape(1, d))


def _mlstm_mixer(x, bsz, t, g_norm, w, b_gate, head_g):
    qk = HEADS * DK
    vv = HEADS * DV
    plan = ((False, ((qk, DK ** -0.5, DK),)), (True, ((qk, 1.0, None), (4 * HEADS, 1.0, None))),
            (False, ((vv, 1.0, DV),)), (False, ((vv, SIGMOID_GATE, DV),)))
    q, kt, gates, v, gate = _inproj(x, g_norm, w, plan, (BF16, BF16, F32, BF16, BF16))
    stats = _mlstm_gates(gates, _gate_rows(b_gate).reshape(4 * HEADS, 1), bsz, t)
    rows = stats.reshape(bsz, 3, 2, HEADS, t).transpose(0, 3, 1, 2, 4).reshape(bsz, HEADS, 6, t)
    return _mlstm_scan(q, kt, v, rows, bsz, t), gate, (0.5 * head_g).reshape(HEADS, 1, DV)


def _gla_mixer(x, bsz, t, g_norm, w, w2, ba, head_g):
    qk = HEADS * DK
    vv = HEADS * DV
    plan = tuple((False, ((n, s, hd),)) for n, s, hd in
                 ((2 * GLA_RANK, 1.0, GLA_GATE), (qk, DK ** -0.5, DK), (qk, 1.0, DK), (vv, 1.0, DV),
                  (vv, SILU_GATE, DV)))
    lg_hi, lg_lo, q, k, v, gate = _inproj(x, g_norm, w, plan, (BF16,) * 6, gate_weights=(w2, ba))
    return _gla_scan(q, k, v, lg_hi, lg_lo, bsz, t), gate, head_g.reshape(HEADS, 1, DV)


def _gate_rows(g):
    g4 = g.reshape((4, HEADS) + g.shape[1:])
    return jnp.concatenate([g4[0], g4[2], g4[1], g4[3]], axis=0)


def _split_cols(w, widths):
    out, s = [], 0
    for n in widths:
        out.append(w[:, s:s + n].astype(BF16))
        s += n
    return out


def _gla_gate_weights(w_a2, b_a):
    wf = w_a2[0].reshape(GLA_RANK, HEADS, DK)
    wb = w_a2[1].reshape(GLA_RANK, HEADS, DK)
    z = jnp.zeros_like(wf)
    w2 = jnp.concatenate([jnp.concatenate([wf, z], axis=2),
                          jnp.concatenate([z, wb], axis=2)], axis=0)
    ba = jnp.concatenate([b_a[0].reshape(HEADS, DK), b_a[1].reshape(HEADS, DK)], axis=1)
    return (w2.reshape(2 * GLA_RANK, HEADS * 2 * DK).astype(BF16),
            ba.reshape(1, HEADS * 2 * DK))


def kernel(x_prompt, x_sample, norm_mix_g, norm_ffn_g, norm_final_g, mlstm_w_in, mlstm_b_gate, mlstm_head_g, mlstm_w_out, gla_w_in, gla_w_a2, gla_b_a, gla_head_g, gla_w_out, ffn_w_gu, ffn_w_down):
    depth = norm_mix_g.shape[0]
    d_ff = ffn_w_down.shape[1]
    qk, vv = HEADS * DK, HEADS * DV
    mlstm_w = [_split_cols(w, (qk, qk, vv, vv, 4 * HEADS)) for w in mlstm_w_in]
    mlstm_w = [[wq, jnp.concatenate([wk.T, _gate_rows(wgt.T)], axis=0), wv, wo]
               for wq, wk, wv, wo, wgt in mlstm_w]
    gla_w = [_split_cols(w, (qk, qk, vv, vv, 2 * GLA_RANK)) for w in gla_w_in]
    gla_w = [[wa, wq, wk, wv, wr] for wq, wk, wv, wr, wa in gla_w]
    gla_gate = [_gla_gate_weights(w, b) for w, b in zip(gla_w_a2, gla_b_a)]
    w_out = [mlstm_w_out.astype(BF16), gla_w_out.astype(BF16)]
    w_gu = ffn_w_gu.astype(BF16)
    w_d = ffn_w_down.astype(BF16)

    outs = []
    for x0 in (x_prompt, x_sample):
        bsz, t, d = x0.shape
        x = x0.reshape(bsz * t, d)
        for i in range(depth):
            j = i // 2
            if i % 2 == 0:
                mixed = _mlstm_mixer(x, bsz, t, norm_mix_g[i], mlstm_w[j], mlstm_b_gate[j],
                                     mlstm_head_g[j])
            else:
                mixed = _gla_mixer(x, bsz, t, norm_mix_g[i], gla_w[j], *gla_gate[j], gla_head_g[j])
            x = _mix_ffn(*mixed, x, w_out[i % 2][j], norm_ffn_g[i], w_gu[i], w_d[i], norm_final_g,
                         final=(i == depth - 1))
        outs.append(x.reshape(bsz, t, d))
    return tuple(outs)
```

```python
import functools

import jax
import jax.numpy as jnp
from jax import lax
from jax.experimental import pallas as pl
from jax.experimental.pallas import tpu as pltpu

F32 = jnp.float32
BF16 = jnp.bfloat16

EPS = 1e-6
LOG2E = 1.4426950408889634
HEADS = 4
DK = 128
DV = 256
LANES = 128
SUBLANES = 8
DVN = DV + LANES
GLA_RANK = 16
GLA_TAU = 16.0
NEG = -1e30

CHUNK = 128
GLA_SUB = 64
STATE_UNROLL = 32
GROUP = 32
GATE_GROUP = 8
TOKEN_TILE = 1024
FFN_CHUNK = 256
VMEM_LIMIT = 56 * 1024 * 1024


def _params(n_parallel):
    return pltpu.CompilerParams(
        dimension_semantics=("parallel",) * n_parallel,
        vmem_limit_bytes=VMEM_LIMIT)


def _rmsnorm(x, g):
    return x * lax.rsqrt(jnp.mean(x * x, axis=-1, keepdims=True) + EPS) * g


def _log_sigmoid(x):
    return jnp.minimum(x, 0.0) - jnp.log(1.0 + jnp.exp(-jnp.abs(x)))


def _dot(a, b):
    return jnp.dot(a, b, preferred_element_type=F32)


def _dot_nt(a, b):
    return lax.dot_general(a, b, (((1,), (1,)), ((), ())), preferred_element_type=F32)


def _dot_tn(a, b):
    return lax.dot_general(a, b, (((0,), (0,)), ((), ())), preferred_element_type=F32)


def _chunk_start(c):
    return c * CHUNK if isinstance(c, int) else pl.multiple_of(c * CHUNK, CHUNK)


def _causal_masks(n):
    ti = lax.broadcasted_iota(jnp.int32, (n, n), 0)
    ji = lax.broadcasted_iota(jnp.int32, (n, n), 1)
    return ji <= ti, ji >= ti


GLA_GATE = "gla_gate"
SIGMOID_GATE = "sigmoid"
SILU_GATE = "silu"
GATE_SLABS = 4


def _write_heads(o_ref, value, head_dim):
    for hd in range(value.shape[1] // head_dim):
        o_ref[hd] = value[:, hd * head_dim:(hd + 1) * head_dim]


def _inproj_kernel(x_ref, g_ref, *refs, plan, n_gate_refs):
    w_refs = refs[:len(plan)]
    gate_refs = list(refs[len(plan):len(plan) + n_gate_refs])
    o_refs = list(refs[len(plan) + n_gate_refs:])
    h = _rmsnorm(x_ref[...], g_ref[...]).astype(BF16)
    pending = []

    def gate_stage(low_rank, w2_ref, ba_ref, hi_ref, lo_ref, rows):
        z = _dot(low_rank[rows].astype(BF16), w2_ref[...]) + ba_ref[...]
        lg = _log_sigmoid(z) * (LOG2E / GLA_TAU)
        hi = lg.astype(BF16)
        lo = (lg - hi.astype(F32)).astype(BF16)
        for hd in range(hi_ref.shape[0]):
            cs = slice(hd * 2 * DK, (hd + 1) * 2 * DK)
            hi_ref[hd, rows, :] = hi[:, cs]
            lo_ref[hd, rows, :] = lo[:, cs]

    for n, (w_ref, (tr, parts)) in enumerate(zip(w_refs, plan)):
        y = _dot_nt(w_ref[...], h) if tr else _dot(h, w_ref[...])
        start = 0
        for width, scale, head_dim in parts:
            part = y[start:start + width, :] if tr else y[:, start:start + width]
            start += width
            if scale == SIGMOID_GATE:
                part = 1.0 + jnp.tanh(0.5 * part)
            elif scale == SILU_GATE:
                half = 0.5 * part
                part = half * (1.0 + jnp.tanh(half))
            elif scale != 1.0:
                part = part * scale
            if head_dim == GLA_GATE:
                stage = (part, gate_refs.pop(0), gate_refs.pop(0), o_refs.pop(0), o_refs.pop(0))
                slab = part.shape[0] // GATE_SLABS
                for r in range(GATE_SLABS):
                    pending.append((n + 1 + r, stage + (slice(r * slab, (r + 1) * slab),)))
                continue
            o_ref = o_refs.pop(0)
            part = part.astype(o_ref.dtype)
            if head_dim is None:
                o_ref[...] = part
            else:
                _write_heads(o_ref, part, head_dim)
        while pending and pending[0][0] <= n:
            gate_stage(*pending.pop(0)[1])
    for _, stage in pending:
        gate_stage(*stage)


def _inproj(x, g, weights, plan, dtypes, gate_weights=()):
    n_tok, d = x.shape
    tm = TOKEN_TILE
    assert n_tok % tm == 0, (n_tok, tm)
    const = lambda i: (0, 0)
    out_specs, out_shape = [], []
    dts = list(dtypes)

    def head_major(nh, head_dim):
        out_specs.append(pl.BlockSpec((nh, tm, head_dim), lambda i: (0, i, 0)))
        out_shape.append(jax.ShapeDtypeStruct((nh, n_tok, head_dim), dts.pop(0)))

    for tr, parts in plan:
        for width, _, head_dim in parts:
            if head_dim == GLA_GATE:
                head_major(HEADS, 2 * DK)
                head_major(HEADS, 2 * DK)
            elif tr:
                out_specs.append(pl.BlockSpec((width, tm), lambda i: (0, i)))
                out_shape.append(jax.ShapeDtypeStruct((width, n_tok), dts.pop(0)))
            elif head_dim is None:
                out_specs.append(pl.BlockSpec((tm, width), lambda i: (i, 0)))
                out_shape.append(jax.ShapeDtypeStruct((n_tok, width), dts.pop(0)))
            else:
                head_major(width // head_dim, head_dim)
    operands = list(weights) + list(gate_weights)
    return pl.pallas_call(
        functools.partial(_inproj_kernel, plan=tuple(plan), n_gate_refs=len(gate_weights)),
        grid=(n_tok // tm,),
        in_specs=[pl.BlockSpec((tm, d), lambda i: (i, 0)),
                  pl.BlockSpec((1, d), const)]
                 + [pl.BlockSpec(w.shape, const) for w in operands],
        out_specs=out_specs,
        out_shape=out_shape,
        compiler_params=_params(1),
        name="inproj",
    )(x, g.reshape(1, d), *operands)


def _scan_lanes(x, op, fill, reverse):
    t = x.shape[-1]
    lane = lax.broadcasted_iota(jnp.int32, x.shape, 1)
    s = 1
    while s < t:
        if reverse:
            x = op(x, jnp.where(lane < t - s, pltpu.roll(x, t - s, axis=1), fill))
        else:
            x = op(x, jnp.where(lane >= s, pltpu.roll(x, s, axis=1), fill))
        s *= 2
    return x


def _mlstm_gates_kernel(g_ref, bias_ref, out_ref):
    n = 2 * HEADS
    lf = _log_sigmoid(g_ref[n:, :] + bias_ref[n:, :])
    fwd = lax.broadcasted_iota(jnp.int32, lf.shape, 0) < HEADS
    b = jnp.where(fwd, _scan_lanes(lf, jnp.add, 0.0, False),
                  _scan_lanes(lf, jnp.add, 0.0, True))
    a = g_ref[:n, :] + bias_ref[:n, :] - b
    amax = jnp.where(fwd, _scan_lanes(a, jnp.maximum, NEG, False),
                     _scan_lanes(a, jnp.maximum, NEG, True))
    out_ref[0, 0] = a * LOG2E
    out_ref[0, 1] = amax * LOG2E
    out_ref[0, 2] = jnp.exp(-(b + amax))


def _mlstm_gates(g, bias, bsz, t):
    return pl.pallas_call(
        _mlstm_gates_kernel,
        grid=(bsz,),
        in_specs=[pl.BlockSpec((4 * HEADS, t), lambda b: (0, b)),
                  pl.BlockSpec((4 * HEADS, 1), lambda b: (0, 0))],
        out_specs=pl.BlockSpec((1, 3, 2 * HEADS, t), lambda b: (b, 0, 0, 0)),
        out_shape=jax.ShapeDtypeStruct((bsz, 3, 2 * HEADS, t), F32),
        compiler_params=_params(1),
        name="mlstm_gates",
    )(g, bias)


def _per_row(x):
    return jnp.broadcast_to(x, (LANES, x.shape[1])).T


def _mlstm_scan_kernel(q_ref, kt_ref, v_ref, row_ref, y_ref,
                       st_ref, cb_ref, ap_ref):
    t = q_ref.shape[1]
    nc = t // CHUNK
    group = min(GROUP, nc)
    masks = _causal_masks(CHUNK)
    ones = jnp.ones((CHUNK, LANES), BF16)

    st_ref[...] = jnp.zeros_like(st_ref)

    def states(i, a_prevs):
        a_ends = []
        for d, a_prev in enumerate(a_prevs):
            c = i if d == 0 else nc - 1 - i
            s = _chunk_start(c)
            kt = kt_ref[:, pl.ds(s, CHUNK)].astype(F32)
            rows = row_ref[0, 0, :, pl.ds(s, CHUNK)]
            edge = CHUNK - 1 if d == 0 else 0
            a_end = rows[2 + d:3 + d, edge:edge + 1]
            kw = kt * jnp.exp2(rows[d:d + 1, :] - a_end)
            va = jnp.concatenate([v_ref[0, pl.ds(s, CHUNK), :], ones], axis=1)
            state = st_ref[d]
            cb_ref[d, c] = state.astype(BF16)
            ap_ref[d, c] = jnp.broadcast_to(a_prev, (SUBLANES, LANES))
            st_ref[d] = jnp.exp2(a_prev - a_end) * state + _dot(kw.astype(BF16), va)
            a_ends.append(a_end)
        return tuple(a_ends)

    lax.fori_loop(0, nc, states, (jnp.full((1, 1), NEG, F32),) * 2, unroll=min(STATE_UNROLL, nc))

    def scores_of(g):
        starts = [_chunk_start(g * group + j) for j in range(group)]
        return [_dot(q_ref[0, pl.ds(s, CHUNK), :], kt_ref[:, pl.ds(s, CHUNK)]) for s in starts]

    def weighted(g, scores):
        lhs = []
        for j, sm in enumerate(scores):
            c = g * group + j
            s = _chunk_start(c)
            rows = row_ref[0, 0, :, pl.ds(s, CHUNK)]
            qf = q_ref[0, pl.ds(s, CHUNK), :].astype(F32)
            for d in (0, 1):
                a_row = rows[d:d + 1, :]
                amax = _per_row(rows[2 + d:3 + d, :])
                a_prev = ap_ref[d, c][0:1, 0:1]
                p = sm * jnp.exp2(jnp.where(masks[d], a_row - amax, NEG))
                sq = qf * jnp.exp2(a_prev - amax)
                lhs.append(jnp.concatenate([p.astype(BF16), sq.astype(BF16)], axis=1))
        return tuple(lhs)

    def readout(g, lhs):
        res = []
        for j in range(group):
            c = g * group + j
            va = jnp.concatenate([v_ref[0, pl.ds(_chunk_start(c), CHUNK), :], ones], axis=1)
            for d in (0, 1):
                res.append(_dot(lhs[2 * j + d], jnp.concatenate([va, cb_ref[d, c]], axis=0)))
        return res

    def finish(g, res):
        for j in range(group):
            s = _chunk_start(g * group + j)
            rows = row_ref[0, 0, :, pl.ds(s, CHUNK)]
            h = None
            for d in (0, 1):
                rd = res[2 * j + d]
                r = 1.0 / jnp.maximum(jnp.abs(rd[:, DV:]), _per_row(rows[4 + d:5 + d, :]))
                hd = rd[:, :DV] * jnp.concatenate([r] * (DV // LANES), axis=1)
                h = hd if h is None else h + hd
            y_ref[0, pl.ds(s, CHUNK), :] = h.astype(y_ref.dtype)

    def step(g, carry):
        finish(g, readout(g, weighted(g, scores_of(g))))
        return carry

    lax.fori_loop(0, nc // group, step, 0)


def _mlstm_scan(q, kt, v, rows, bsz, t):
    assert t % CHUNK == 0 and (t // CHUNK) % min(GROUP, t // CHUNK) == 0, t
    nc = t // CHUNK
    v_spec = pl.BlockSpec((1, t, DV), lambda b, h: (h, b, 0))
    return pl.pallas_call(
        _mlstm_scan_kernel,
        grid=(bsz, HEADS),
        in_specs=[pl.BlockSpec((1, t, DK), lambda b, h: (h, b, 0)),
                  pl.BlockSpec((DK, t), lambda b, h: (h, b)),
                  v_spec,
                  pl.BlockSpec((1, 1, 6, t), lambda b, h: (b, h, 0, 0))],
        out_specs=v_spec,
        out_shape=jax.ShapeDtypeStruct((HEADS, bsz * t, DV), BF16),
        scratch_shapes=[pltpu.VMEM((2, DK, DVN), F32), pltpu.VMEM((2, nc, DK, DVN), BF16),
                        pltpu.VMEM((2, nc, SUBLANES, LANES), F32)],
        compiler_params=_params(2),
        name="mlstm_scan",
    )(q, kt, v, rows)


def _gla_scan_kernel(q_ref, k_ref, v_ref, lgh_ref, lgl_ref, y_ref,
                     bc_ref, st_ref, sb_ref):
    t = q_ref.shape[1]
    nc = t // CHUNK
    group = min(GROUP, nc)
    sub = GLA_SUB
    masks = _causal_masks(CHUNK)
    tri = tuple(m.astype(BF16) for m in masks)

    gate_rows = GATE_GROUP * CHUNK

    def cumulate(i, carry):
        s = pl.multiple_of(i * gate_rows, gate_rows)
        hi = lgh_ref[0, pl.ds(s, gate_rows), :]
        lo = lgl_ref[0, pl.ds(s, gate_rows), :]
        parts = []
        for j in range(GATE_GROUP):
            rs = slice(j * CHUNK, (j + 1) * CHUNK)
            for d in (0, 1):
                cs = slice(d * DK, (d + 1) * DK)
                parts.append(_dot(tri[d], jnp.concatenate([hi[rs, cs], lo[rs, cs]], axis=1)))
        for j in range(GATE_GROUP):
            for d in (0, 1):
                pd = parts[2 * j + d]
                bc_ref[pl.ds(s + j * CHUNK, CHUNK), d * DK:(d + 1) * DK] = pd[:, :DK] + pd[:, DK:]
        return carry

    lax.fori_loop(0, nc // GATE_GROUP, cumulate, 0)

    st_ref[...] = jnp.zeros_like(st_ref)

    def states(i, carry):
        for d in (0, 1):
            c = i if d == 0 else nc - 1 - i
            s = _chunk_start(c)
            kc = k_ref[0, pl.ds(s, CHUNK), :].astype(F32)
            bc = bc_ref[pl.ds(s, CHUNK), d * DK:(d + 1) * DK]
            edge = CHUNK - 1 if d == 0 else 0
            b_last = bc[edge:edge + 1, :]
            kd = kc * jnp.exp2(b_last - bc)
            decay = jnp.concatenate([_per_row(jnp.exp2(b_last))] * (DV // LANES), axis=1)
            state = st_ref[d]
            sb_ref[d, c] = state.astype(BF16)
            st_ref[d] = decay * state + _dot_tn(kd.astype(BF16), v_ref[0, pl.ds(s, CHUNK), :])
        return carry

    lax.fori_loop(0, nc, states, 0, unroll=min(STATE_UNROLL, nc))

    zeros = jnp.zeros((sub, DK), BF16)

    def scores_of(g):
        starts = [_chunk_start(g * group + j) for j in range(group)]
        scores, q_state = [], []
        for s in starts:
            qc = q_ref[0, pl.ds(s, CHUNK), :].astype(F32)
            kc = k_ref[0, pl.ds(s, CHUNK), :].astype(F32)
            for d in (0, 1):
                bc = bc_ref[pl.ds(s, CHUNK), d * DK:(d + 1) * DK]
                first, second = ((slice(0, sub), slice(sub, CHUNK)) if d == 0
                                 else (slice(sub, CHUNK), slice(0, sub)))
                ref_row = sub - 1 if d == 0 else sub
                r = bc[ref_row:ref_row + 1, :]
                qe = (qc * jnp.exp2(bc)).astype(BF16)
                qg = (qc[second] * jnp.exp2(bc[second] - r)).astype(BF16)
                kf = (kc[first] * jnp.exp2(-bc[first])).astype(BF16)
                kg = (kc * jnp.exp2(r - bc)).astype(BF16)
                q_first = jnp.concatenate([qe[first], zeros], axis=1)
                q_second = jnp.concatenate([zeros, qg], axis=1)
                if d == 0:
                    lhs = jnp.concatenate([q_first, q_second], axis=0)
                    kf_all = jnp.concatenate([kf, zeros], axis=0)
                else:
                    lhs = jnp.concatenate([q_second, q_first], axis=0)
                    kf_all = jnp.concatenate([zeros, kf], axis=0)
                keys = jnp.concatenate([kf_all, kg], axis=1)
                scores.append(_dot_nt(lhs, keys))
                q_state.append(qe)
        return scores, tuple(q_state)

    def masked(scores):
        return tuple(jnp.where(masks[n % 2], sc, 0.0).astype(BF16) for n, sc in enumerate(scores))

    def readout(g, amats, q_state):
        outs = []
        for j in range(group):
            c = g * group + j
            vc = v_ref[0, pl.ds(_chunk_start(c), CHUNK), :]
            o = None
            for d in (0, 1):
                od = _dot(jnp.concatenate([q_state[2 * j + d], amats[2 * j + d]], axis=1),
                          jnp.concatenate([sb_ref[d, c], vc], axis=0))
                o = od if o is None else o + od
            outs.append(o)
        return outs

    def step(g, carry):
        scores, q_state = scores_of(g)
        for j, o in enumerate(readout(g, masked(scores), q_state)):
            y_ref[0, pl.ds(_chunk_start(g * group + j), CHUNK), :] = o.astype(y_ref.dtype)
        return carry

    lax.fori_loop(0, nc // group, step, 0)


def _gla_scan(q, k, v, lg_hi, lg_lo, bsz, t):
    assert t % (CHUNK * GATE_GROUP) == 0 and (t // CHUNK) % min(GROUP, t // CHUNK) == 0, t
    nc = t // CHUNK
    qk_spec = pl.BlockSpec((1, t, DK), lambda b, h: (h, b, 0))
    v_spec = pl.BlockSpec((1, t, DV), lambda b, h: (h, b, 0))
    return pl.pallas_call(
        _gla_scan_kernel,
        grid=(bsz, HEADS),
        in_specs=[qk_spec, qk_spec, v_spec, v_spec, v_spec],
        out_specs=v_spec,
        out_shape=jax.ShapeDtypeStruct((HEADS, bsz * t, DV), BF16),
        scratch_shapes=[pltpu.VMEM((t, 2 * DK), F32),
                        pltpu.VMEM((2, DK, DV), F32), pltpu.VMEM((2, nc, DK, DV), BF16)],
        compiler_params=_params(2),
        name="gla_scan",
    )(q, k, v, lg_hi, lg_lo)


def _mix_ffn_kernel(hs_ref, gate_ref, hg_ref, x_ref, wo_ref, gn_ref, wgu_ref, wd_ref, gl_ref, out_ref,
                    *, final):
    heads = [(_rmsnorm(hs_ref[hd].astype(F32), hg_ref[hd]) * gate_ref[hd].astype(F32)).astype(BF16)
             for hd in range(hs_ref.shape[0])]
    x1 = x_ref[...] + _dot(jnp.concatenate(heads, axis=1), wo_ref[...])
    h = _rmsnorm(x1, gn_ref[...]).astype(BF16)
    acc = x1
    d_ff = wd_ref.shape[0]
    for c in range(d_ff // FFN_CHUNK):
        cs = slice(c * FFN_CHUNK, (c + 1) * FFN_CHUNK)
        g = _dot(h, wgu_ref[:, cs])
        u = _dot(h, wgu_ref[:, d_ff + c * FFN_CHUNK:d_ff + (c + 1) * FFN_CHUNK])
        acc = acc + _dot((jax.nn.silu(g) * u).astype(BF16), wd_ref[cs, :])
    if final:
        acc = _rmsnorm(acc, gl_ref[...])
    out_ref[...] = acc


def _mix_ffn(hs, gate, head_g, x, wo, gn, wgu, wd, gl, final):
    n_tok, d = x.shape
    tm = TOKEN_TILE
    assert n_tok % tm == 0 and wd.shape[0] % FFN_CHUNK == 0, (n_tok, wd.shape)
    const = lambda i: (0, 0)
    resident = lambda w: pl.BlockSpec(w.shape, const, pipeline_mode=pl.Buffered(1))
    tile = pl.BlockSpec((tm, d), lambda i: (i, 0))
    head_tile = pl.BlockSpec((hs.shape[0], tm, hs.shape[2]), lambda i: (0, i, 0))
    vec = pl.BlockSpec((1, d), const)
    return pl.pallas_call(
        functools.partial(_mix_ffn_kernel, final=final),
        grid=(n_tok // tm,),
        in_specs=[head_tile, head_tile, pl.BlockSpec(head_g.shape, lambda i: (0, 0, 0)), tile,
                  resident(wo), vec, resident(wgu), resident(wd), vec],
        out_specs=tile,
        out_shape=jax.ShapeDtypeStruct((n_tok, d), F32),
        compiler_params=_params(1),
        name="mix_ffn",
    )(hs, gate, head_g, x, wo, gn.reshape(1, d), wgu, wd, gl.reshape(1, d))


def _mlstm_mixer(x, bsz, t, g_norm, w, b_gate, head_g):
    qk = HEADS * DK
    vv = HEADS * DV
    plan = ((False, ((qk, DK ** -0.5, DK),)), (True, ((qk, 1.0, None), (4 * HEADS, 1.0, None))),
            (False, ((vv, 1.0, DV),)), (False, ((vv, SIGMOID_GATE, DV),)))
    q, kt, gates, v, gate = _inproj(x, g_norm, w, plan, (BF16, BF16, F32, BF16, BF16))
    stats = _mlstm_gates(gates, _gate_rows(b_gate).reshape(4 * HEADS, 1), bsz, t)
    rows = stats.reshape(bsz, 3, 2, HEADS, t).transpose(0, 3, 1, 2, 4).reshape(bsz, HEADS, 6, t)
    return _mlstm_scan(q, kt, v, rows, bsz, t), gate, (0.5 * head_g).reshape(HEADS, 1, DV)


def _gla_mixer(x, bsz, t, g_norm, w, w2, ba, head_g):
    qk = HEADS * DK
    vv = HEADS * DV
    plan = tuple((False, ((n, s, hd),)) for n, s, hd in
                 ((2 * GLA_RANK, 1.0, GLA_GATE), (qk, DK ** -0.5, DK), (qk, 1.0, DK), (vv, 1.0, DV),
                  (vv, SILU_GATE, DV)))
    lg_hi, lg_lo, q, k, v, gate = _inproj(x, g_norm, w, plan, (BF16,) * 6, gate_weights=(w2, ba))
    return _gla_scan(q, k, v, lg_hi, lg_lo, bsz, t), gate, head_g.reshape(HEADS, 1, DV)


def _gate_rows(g):
    g4 = g.reshape((4, HEADS) + g.shape[1:])
    return jnp.concatenate([g4[0], g4[2], g4[1], g4[3]], axis=0)


def _split_cols(w, widths):
    out, s = [], 0
    for n in widths:
        out.append(w[:, s:s + n].astype(BF16))
        s += n
    return out


def _gla_gate_weights(w_a2, b_a):
    wf = w_a2[0].reshape(GLA_RANK, HEADS, DK)
    wb = w_a2[1].reshape(GLA_RANK, HEADS, DK)
    z = jnp.zeros_like(wf)
    w2 = jnp.concatenate([jnp.concatenate([wf, z], axis=2),
                          jnp.concatenate([z, wb], axis=2)], axis=0)
    ba = jnp.concatenate([b_a[0].reshape(HEADS, DK), b_a[1].reshape(HEADS, DK)], axis=1)
    return (w2.reshape(2 * GLA_RANK, HEADS * 2 * DK).astype(BF16),
            ba.reshape(1, HEADS * 2 * DK))


def kernel(x_prompt, x_sample, norm_mix_g, norm_ffn_g, norm_final_g, mlstm_w_in, mlstm_b_gate, mlstm_head_g, mlstm_w_out, gla_w_in, gla_w_a2, gla_b_a, gla_head_g, gla_w_out, ffn_w_gu, ffn_w_down):
    depth = norm_mix_g.shape[0]
    d_ff = ffn_w_down.shape[1]
    qk, vv = HEADS * DK, HEADS * DV
    mlstm_w = [_split_cols(w, (qk, qk, vv, vv, 4 * HEADS)) for w in mlstm_w_in]
    mlstm_w = [[wq, jnp.concatenate([wk.T, _gate_rows(wgt.T)], axis=0), wv, wo]
               for wq, wk, wv, wo, wgt in mlstm_w]
    gla_w = [_split_cols(w, (qk, qk, vv, vv, 2 * GLA_RANK)) for w in gla_w_in]
    gla_w = [[wa, wq, wk, wv, wr] for wq, wk, wv, wr, wa in gla_w]
    gla_gate = [_gla_gate_weights(w, b) for w, b in zip(gla_w_a2, gla_b_a)]
    w_out = [mlstm_w_out.astype(BF16), gla_w_out.astype(BF16)]
    w_gu = ffn_w_gu.astype(BF16)
    w_d = ffn_w_down.astype(BF16)

    outs = []
    for x0 in (x_prompt, x_sample):
        bsz, t, d = x0.shape
        x = x0.reshape(bsz * t, d)
        for i in range(depth):
            j = i // 2
            if i % 2 == 0:
                mixed = _mlstm_mixer(x, bsz, t, norm_mix_g[i], mlstm_w[j], mlstm_b_gate[j],
                                     mlstm_head_g[j])
            else:
                mixed = _gla_mixer(x, bsz, t, norm_mix_g[i], gla_w[j], *gla_gate[j], gla_head_g[j])
            x = _mix_ffn(*mixed, x, w_out[i % 2][j], norm_ffn_g[i], w_gu[i], w_d[i], norm_final_g,
                         final=(i == depth - 1))
        outs.append(x.reshape(bsz, t, d))
    return tuple(outs)
```

```python
import functools

import jax
import jax.numpy as jnp
from jax import lax
from jax.experimental import pallas as pl
from jax.experimental.pallas import tpu as pltpu

F32 = jnp.float32
BF16 = jnp.bfloat16

EPS = 1e-6
LOG2E = 1.4426950408889634
HEADS = 4
DK = 128
DV = 256
LANES = 128
SUBLANES = 8
DVN = DV + LANES
GLA_RANK = 16
GLA_TAU = 16.0
NEG = -1e30

CHUNK = 128
GLA_SUB = 64
STATE_UNROLL = 32
GROUP = 32
GATE_GROUP = 32
TOKEN_TILE = 1024
FFN_CHUNK = 256
VMEM_LIMIT = 56 * 1024 * 1024


def _params(n_parallel):
    return pltpu.CompilerParams(
        dimension_semantics=("parallel",) * n_parallel,
        vmem_limit_bytes=VMEM_LIMIT)


def _rmsnorm(x, g):
    return x * lax.rsqrt(jnp.mean(x * x, axis=-1, keepdims=True) + EPS) * g


def _log_sigmoid(x):
    return jnp.minimum(x, 0.0) - jnp.log(1.0 + jnp.exp(-jnp.abs(x)))


def _dot(a, b):
    return jnp.dot(a, b, preferred_element_type=F32)


def _dot_nt(a, b):
    return lax.dot_general(a, b, (((1,), (1,)), ((), ())), preferred_element_type=F32)


def _dot_tn(a, b):
    return lax.dot_general(a, b, (((0,), (0,)), ((), ())), preferred_element_type=F32)


def _chunk_start(c):
    return c * CHUNK if isinstance(c, int) else pl.multiple_of(c * CHUNK, CHUNK)


def _causal_masks(n):
    ti = lax.broadcasted_iota(jnp.int32, (n, n), 0)
    ji = lax.broadcasted_iota(jnp.int32, (n, n), 1)
    return ji <= ti, ji >= ti


GLA_GATE = "gla_gate"
SIGMOID_GATE = "sigmoid"
SILU_GATE = "silu"
GATE_SLABS = 4


def _write_heads(o_ref, value, head_dim):
    for hd in range(value.shape[1] // head_dim):
        o_ref[hd] = value[:, hd * head_dim:(hd + 1) * head_dim]


def _inproj_kernel(x_ref, g_ref, *refs, plan, n_gate_refs):
    w_refs = refs[:len(plan)]
    gate_refs = list(refs[len(plan):len(plan) + n_gate_refs])
    o_refs = list(refs[len(plan) + n_gate_refs:])
    h = _rmsnorm(x_ref[...], g_ref[...]).astype(BF16)
    pending = []

    def gate_stage(low_rank, w2_ref, ba_ref, hi_ref, lo_ref, rows):
        z = _dot(low_rank[rows].astype(BF16), w2_ref[...]) + ba_ref[...]
        lg = _log_sigmoid(z) * (LOG2E / GLA_TAU)
        hi = lg.astype(BF16)
        lo = (lg - hi.astype(F32)).astype(BF16)
        for hd in range(hi_ref.shape[0]):
            cs = slice(hd * 2 * DK, (hd + 1) * 2 * DK)
            hi_ref[hd, rows, :] = hi[:, cs]
            lo_ref[hd, rows, :] = lo[:, cs]

    for n, (w_ref, (tr, parts)) in enumerate(zip(w_refs, plan)):
        y = _dot_nt(w_ref[...], h) if tr else _dot(h, w_ref[...])
        start = 0
        for width, scale, head_dim in parts:
            part = y[start:start + width, :] if tr else y[:, start:start + width]
            start += width
            if scale == SIGMOID_GATE:
                part = 1.0 + jnp.tanh(0.5 * part)
            elif scale == SILU_GATE:
                half = 0.5 * part
                part = half * (1.0 + jnp.tanh(half))
            elif scale != 1.0:
                part = part * scale
            if head_dim == GLA_GATE:
                stage = (part, gate_refs.pop(0), gate_refs.pop(0), o_refs.pop(0), o_refs.pop(0))
                slab = part.shape[0] // GATE_SLABS
                for r in range(GATE_SLABS):
                    pending.append((n + 1 + r, stage + (slice(r * slab, (r + 1) * slab),)))
                continue
            o_ref = o_refs.pop(0)
            part = part.astype(o_ref.dtype)
            if head_dim is None:
                o_ref[...] = part
            else:
                _write_heads(o_ref, part, head_dim)
        while pending and pending[0][0] <= n:
            gate_stage(*pending.pop(0)[1])
    for _, stage in pending:
        gate_stage(*stage)


def _inproj(x, g, weights, plan, dtypes, gate_weights=()):
    n_tok, d = x.shape
    tm = TOKEN_TILE
    assert n_tok % tm == 0, (n_tok, tm)
    const = lambda i: (0, 0)
    out_specs, out_shape = [], []
    dts = list(dtypes)

    def head_major(nh, head_dim):
        out_specs.append(pl.BlockSpec((nh, tm, head_dim), lambda i: (0, i, 0)))
        out_shape.append(jax.ShapeDtypeStruct((nh, n_tok, head_dim), dts.pop(0)))

    for tr, parts in plan:
        for width, _, head_dim in parts:
            if head_dim == GLA_GATE:
                head_major(HEADS, 2 * DK)
                head_major(HEADS, 2 * DK)
            elif tr:
                out_specs.append(pl.BlockSpec((width, tm), lambda i: (0, i)))
                out_shape.append(jax.ShapeDtypeStruct((width, n_tok), dts.pop(0)))
            elif head_dim is None:
                out_specs.append(pl.BlockSpec((tm, width), lambda i: (i, 0)))
                out_shape.append(jax.ShapeDtypeStruct((n_tok, width), dts.pop(0)))
            else:
                head_major(width // head_dim, head_dim)
    operands = list(weights) + list(gate_weights)
    return pl.pallas_call(
        functools.partial(_inproj_kernel, plan=tuple(plan), n_gate_refs=len(gate_weights)),
        grid=(n_tok // tm,),
        in_specs=[pl.BlockSpec((tm, d), lambda i: (i, 0)),
                  pl.BlockSpec((1, d), const)]
                 + [pl.BlockSpec(w.shape, const) for w in operands],
        out_specs=out_specs,
        out_shape=out_shape,
        compiler_params=_params(1),
        name="inproj",
    )(x, g.reshape(1, d), *operands)


def _scan_lanes(x, op, fill, reverse):
    t = x.shape[-1]
    lane = lax.broadcasted_iota(jnp.int32, x.shape, 1)
    s = 1
    while s < t:
        if reverse:
            x = op(x, jnp.where(lane < t - s, pltpu.roll(x, t - s, axis=1), fill))
        else:
            x = op(x, jnp.where(lane >= s, pltpu.roll(x, s, axis=1), fill))
        s *= 2
    return x


def _mlstm_gates_kernel(g_ref, bias_ref, out_ref):
    n = 2 * HEADS
    lf = _log_sigmoid(g_ref[n:, :] + bias_ref[n:, :])
    fwd = lax.broadcasted_iota(jnp.int32, lf.shape, 0) < HEADS
    b = jnp.where(fwd, _scan_lanes(lf, jnp.add, 0.0, False),
                  _scan_lanes(lf, jnp.add, 0.0, True))
    a = g_ref[:n, :] + bias_ref[:n, :] - b
    amax = jnp.where(fwd, _scan_lanes(a, jnp.maximum, NEG, False),
                     _scan_lanes(a, jnp.maximum, NEG, True))
    out_ref[0, 0] = a * LOG2E
    out_ref[0, 1] = amax * LOG2E
    out_ref[0, 2] = jnp.exp(-(b + amax))


def _mlstm_gates(g, bias, bsz, t):
    return pl.pallas_call(
        _mlstm_gates_kernel,
        grid=(bsz,),
        in_specs=[pl.BlockSpec((4 * HEADS, t), lambda b: (0, b)),
                  pl.BlockSpec((4 * HEADS, 1), lambda b: (0, 0))],
        out_specs=pl.BlockSpec((1, 3, 2 * HEADS, t), lambda b: (b, 0, 0, 0)),
        out_shape=jax.ShapeDtypeStruct((bsz, 3, 2 * HEADS, t), F32),
        compiler_params=_params(1),
        name="mlstm_gates",
    )(g, bias)


def _per_row(x):
    return jnp.broadcast_to(x, (LANES, x.shape[1])).T


def _mlstm_scan_kernel(q_ref, kt_ref, v_ref, row_ref, y_ref,
                       st_ref, cb_ref, ap_ref):
    t = q_ref.shape[1]
    nc = t // CHUNK
    group = min(GROUP, nc)
    masks = _causal_masks(CHUNK)
    ones = jnp.ones((CHUNK, LANES), BF16)

    st_ref[...] = jnp.zeros_like(st_ref)

    def states(i, a_prevs):
        a_ends = []
        for d, a_prev in enumerate(a_prevs):
            c = i if d == 0 else nc - 1 - i
            s = _chunk_start(c)
            kt = kt_ref[:, pl.ds(s, CHUNK)].astype(F32)
            rows = row_ref[0, 0, :, pl.ds(s, CHUNK)]
            edge = CHUNK - 1 if d == 0 else 0
            a_end = rows[2 + d:3 + d, edge:edge + 1]
            kw = kt * jnp.exp2(rows[d:d + 1, :] - a_end)
            va = jnp.concatenate([v_ref[0, pl.ds(s, CHUNK), :], ones], axis=1)
            state = st_ref[d]
            cb_ref[d, c] = state.astype(BF16)
            ap_ref[d, c] = jnp.broadcast_to(a_prev, (SUBLANES, LANES))
            st_ref[d] = jnp.exp2(a_prev - a_end) * state + _dot(kw.astype(BF16), va)
            a_ends.append(a_end)
        return tuple(a_ends)

    lax.fori_loop(0, nc, states, (jnp.full((1, 1), NEG, F32),) * 2, unroll=min(STATE_UNROLL, nc))

    def scores_of(g):
        starts = [_chunk_start(g * group + j) for j in range(group)]
        return [_dot(q_ref[0, pl.ds(s, CHUNK), :], kt_ref[:, pl.ds(s, CHUNK)]) for s in starts]

    def weighted(g, scores):
        lhs = []
        for j, sm in enumerate(scores):
            c = g * group + j
            s = _chunk_start(c)
            rows = row_ref[0, 0, :, pl.ds(s, CHUNK)]
            qf = q_ref[0, pl.ds(s, CHUNK), :].astype(F32)
            for d in (0, 1):
                a_row = rows[d:d + 1, :]
                amax = _per_row(rows[2 + d:3 + d, :])
                a_prev = ap_ref[d, c][0:1, 0:1]
                p = sm * jnp.exp2(jnp.where(masks[d], a_row - amax, NEG))
                sq = qf * jnp.exp2(a_prev - amax)
                lhs.append(jnp.concatenate([p.astype(BF16), sq.astype(BF16)], axis=1))
        return tuple(lhs)

    def readout(g, lhs):
        res = []
        for j in range(group):
            c = g * group + j
            va = jnp.concatenate([v_ref[0, pl.ds(_chunk_start(c), CHUNK), :], ones], axis=1)
            for d in (0, 1):
                res.append(_dot(lhs[2 * j + d], jnp.concatenate([va, cb_ref[d, c]], axis=0)))
        return res

    def finish(g, res):
        for j in range(group):
            s = _chunk_start(g * group + j)
            rows = row_ref[0, 0, :, pl.ds(s, CHUNK)]
            h = None
            for d in (0, 1):
                rd = res[2 * j + d]
                r = 1.0 / jnp.maximum(jnp.abs(rd[:, DV:]), _per_row(rows[4 + d:5 + d, :]))
                hd = rd[:, :DV] * jnp.concatenate([r] * (DV // LANES), axis=1)
                h = hd if h is None else h + hd
            y_ref[0, pl.ds(s, CHUNK), :] = h.astype(y_ref.dtype)

    def step(g, carry):
        finish(g, readout(g, weighted(g, scores_of(g))))
        return carry

    lax.fori_loop(0, nc // group, step, 0)


def _mlstm_scan(q, kt, v, rows, bsz, t):
    assert t % CHUNK == 0 and (t // CHUNK) % min(GROUP, t // CHUNK) == 0, t
    nc = t // CHUNK
    v_spec = pl.BlockSpec((1, t, DV), lambda b, h: (h, b, 0))
    return pl.pallas_call(
        _mlstm_scan_kernel,
        grid=(bsz, HEADS),
        in_specs=[pl.BlockSpec((1, t, DK), lambda b, h: (h, b, 0)),
                  pl.BlockSpec((DK, t), lambda b, h: (h, b)),
                  v_spec,
                  pl.BlockSpec((1, 1, 6, t), lambda b, h: (b, h, 0, 0))],
        out_specs=v_spec,
        out_shape=jax.ShapeDtypeStruct((HEADS, bsz * t, DV), BF16),
        scratch_shapes=[pltpu.VMEM((2, DK, DVN), F32), pltpu.VMEM((2, nc, DK, DVN), BF16),
                        pltpu.VMEM((2, nc, SUBLANES, LANES), F32)],
        compiler_params=_params(2),
        name="mlstm_scan",
    )(q, kt, v, rows)


def _gla_scan_kernel(q_ref, k_ref, v_ref, lgh_ref, lgl_ref, y_ref,
                     bc_ref, st_ref, sb_ref):
    t = q_ref.shape[1]
    nc = t // CHUNK
    group = min(GROUP, nc)
    sub = GLA_SUB
    masks = _causal_masks(CHUNK)
    tri = tuple(m.astype(BF16) for m in masks)

    gate_group = min(GATE_GROUP, nc)
    gate_rows = gate_group * CHUNK

    def cumulate(i, carry):
        s = pl.multiple_of(i * gate_rows, gate_rows)
        hi = lgh_ref[0, pl.ds(s, gate_rows), :]
        lo = lgl_ref[0, pl.ds(s, gate_rows), :]
        parts = []
        for j in range(gate_group):
            rs = slice(j * CHUNK, (j + 1) * CHUNK)
            for d in (0, 1):
                cs = slice(d * DK, (d + 1) * DK)
                parts.append(_dot(tri[d], jnp.concatenate([hi[rs, cs], lo[rs, cs]], axis=1)))
        for j in range(gate_group):
            for d in (0, 1):
                pd = parts[2 * j + d]
                bc_ref[pl.ds(s + j * CHUNK, CHUNK), d * DK:(d + 1) * DK] = pd[:, :DK] + pd[:, DK:]
        return carry

    lax.fori_loop(0, nc // gate_group, cumulate, 0)

    st_ref[...] = jnp.zeros_like(st_ref)

    def states(i, carry):
        for d in (0, 1):
            c = i if d == 0 else nc - 1 - i
            s = _chunk_start(c)
            kc = k_ref[0, pl.ds(s, CHUNK), :].astype(F32)
            bc = bc_ref[pl.ds(s, CHUNK), d * DK:(d + 1) * DK]
            edge = CHUNK - 1 if d == 0 else 0
            b_last = bc[edge:edge + 1, :]
            kd = kc * jnp.exp2(b_last - bc)
            decay = jnp.concatenate([_per_row(jnp.exp2(b_last))] * (DV // LANES), axis=1)
            state = st_ref[d]
            sb_ref[d, c] = state.astype(BF16)
            st_ref[d] = decay * state + _dot_tn(kd.astype(BF16), v_ref[0, pl.ds(s, CHUNK), :])
        return carry

    lax.fori_loop(0, nc, states, 0, unroll=min(STATE_UNROLL, nc))

    zeros = jnp.zeros((sub, DK), BF16)

    def scores_of(g):
        starts = [_chunk_start(g * group + j) for j in range(group)]
        scores, q_state = [], []
        for s in starts:
            qc = q_ref[0, pl.ds(s, CHUNK), :].astype(F32)
            kc = k_ref[0, pl.ds(s, CHUNK), :].astype(F32)
            for d in (0, 1):
                bc = bc_ref[pl.ds(s, CHUNK), d * DK:(d + 1) * DK]
                first, second = ((slice(0, sub), slice(sub, CHUNK)) if d == 0
                                 else (slice(sub, CHUNK), slice(0, sub)))
                ref_row = sub - 1 if d == 0 else sub
                r = bc[ref_row:ref_row + 1, :]
                qe = (qc * jnp.exp2(bc)).astype(BF16)
                qg = (qc[second] * jnp.exp2(bc[second] - r)).astype(BF16)
                kf = (kc[first] * jnp.exp2(-bc[first])).astype(BF16)
                kg = (kc * jnp.exp2(r - bc)).astype(BF16)
                q_first = jnp.concatenate([qe[first], zeros], axis=1)
                q_second = jnp.concatenate([zeros, qg], axis=1)
                if d == 0:
                    lhs = jnp.concatenate([q_first, q_second], axis=0)
                    kf_all = jnp.concatenate([kf, zeros], axis=0)
                else:
                    lhs = jnp.concatenate([q_second, q_first], axis=0)
                    kf_all = jnp.concatenate([zeros, kf], axis=0)
                keys = jnp.concatenate([kf_all, kg], axis=1)
                scores.append(_dot_nt(lhs, keys))
                q_state.append(qe)
        return scores, tuple(q_state)

    def masked(scores):
        return tuple(jnp.where(masks[n % 2], sc, 0.0).astype(BF16) for n, sc in enumerate(scores))

    def readout(g, amats, q_state):
        outs = []
        for j in range(group):
            c = g * group + j
            vc = v_ref[0, pl.ds(_chunk_start(c), CHUNK), :]
            o = None
            for d in (0, 1):
                od = _dot(jnp.concatenate([q_state[2 * j + d], amats[2 * j + d]], axis=1),
                          jnp.concatenate([sb_ref[d, c], vc], axis=0))
                o = od if o is None else o + od
            outs.append(o)
        return outs

    def step(g, carry):
        scores, q_state = scores_of(g)
        for j, o in enumerate(readout(g, masked(scores), q_state)):
            y_ref[0, pl.ds(_chunk_start(g * group + j), CHUNK), :] = o.astype(y_ref.dtype)
        return carry

    lax.fori_loop(0, nc // group, step, 0)


def _gla_scan(q, k, v, lg_hi, lg_lo, bsz, t):
    nc = t // CHUNK
    assert t % CHUNK == 0 and nc % min(GROUP, nc) == 0 and nc % min(GATE_GROUP, nc) == 0, t
    qk_spec = pl.BlockSpec((1, t, DK), lambda b, h: (h, b, 0))
    v_spec = pl.BlockSpec((1, t, DV), lambda b, h: (h, b, 0))
    return pl.pallas_call(
        _gla_scan_kernel,
        grid=(bsz, HEADS),
        in_specs=[qk_spec, qk_spec, v_spec, v_spec, v_spec],
        out_specs=v_spec,
        out_shape=jax.ShapeDtypeStruct((HEADS, bsz * t, DV), BF16),
        scratch_shapes=[pltpu.VMEM((t, 2 * DK), F32),
                        pltpu.VMEM((2, DK, DV), F32), pltpu.VMEM((2, nc, DK, DV), BF16)],
        compiler_params=_params(2),
        name="gla_scan",
    )(q, k, v, lg_hi, lg_lo)


def _mix_ffn_kernel(hs_ref, gate_ref, hg_ref, x_ref, wo_ref, gn_ref, wgu_ref, wd_ref, gl_ref, out_ref,
                    *, final):
    heads = [(_rmsnorm(hs_ref[hd].astype(F32), hg_ref[hd]) * gate_ref[hd].astype(F32)).astype(BF16)
             for hd in range(hs_ref.shape[0])]
    x1 = x_ref[...] + _dot(jnp.concatenate(heads, axis=1), wo_ref[...])
    h = _rmsnorm(x1, gn_ref[...]).astype(BF16)
    acc = x1
    d_ff = wd_ref.shape[0]
    for c in range(d_ff // FFN_CHUNK):
        cs = slice(c * FFN_CHUNK, (c + 1) * FFN_CHUNK)
        g = _dot(h, wgu_ref[:, cs])
        u = _dot(h, wgu_ref[:, d_ff + c * FFN_CHUNK:d_ff + (c + 1) * FFN_CHUNK])
        acc = acc + _dot((jax.nn.silu(g) * u).astype(BF16), wd_ref[cs, :])
    if final:
        acc = _rmsnorm(acc, gl_ref[...])
    out_ref[...] = acc


def _mix_ffn(hs, gate, head_g, x, wo, gn, wgu, wd, gl, final):
    n_tok, d = x.shape
    tm = TOKEN_TILE
    assert n_tok % tm == 0 and wd.shape[0] % FFN_CHUNK == 0, (n_tok, wd.shape)
    const = lambda i: (0, 0)
    resident = lambda w: pl.BlockSpec(w.shape, const, pipeline_mode=pl.Buffered(1))
    tile = pl.BlockSpec((tm, d), lambda i: (i, 0))
    head_tile = pl.BlockSpec((hs.shape[0], tm, hs.shape[2]), lambda i: (0, i, 0))
    vec = pl.BlockSpec((1, d), const)
    return pl.pallas_call(
        functools.partial(_mix_ffn_kernel, final=final),
        grid=(n_tok // tm,),
        in_specs=[head_tile, head_tile, pl.BlockSpec(head_g.shape, lambda i: (0, 0, 0)), tile,
                  resident(wo), vec, resident(wgu), resident(wd), vec],
        out_specs=tile,
        out_shape=jax.ShapeDtypeStruct((n_tok, d), F32),
        compiler_params=_params(1),
        name="mix_ffn",
    )(hs, gate, head_g, x, wo, gn.reshape(1, d), wgu, wd, gl.reshape(1, d))


def _mlstm_mixer(x, bsz, t, g_norm, w, b_gate, head_g):
    qk = HEADS * DK
    vv = HEADS * DV
    plan = ((False, ((qk, DK ** -0.5, DK),)), (True, ((qk, 1.0, None), (4 * HEADS, 1.0, None))),
            (False, ((vv, 1.0, DV),)), (False, ((vv, SIGMOID_GATE, DV),)))
    q, kt, gates, v, gate = _inproj(x, g_norm, w, plan, (BF16, BF16, F32, BF16, BF16))
    stats = _mlstm_gates(gates, _gate_rows(b_gate).reshape(4 * HEADS, 1), bsz, t)
    rows = stats.reshape(bsz, 3, 2, HEADS, t).transpose(0, 3, 1, 2, 4).reshape(bsz, HEADS, 6, t)
    return _mlstm_scan(q, kt, v, rows, bsz, t), gate, (0.5 * head_g).reshape(HEADS, 1, DV)


def _gla_mixer(x, bsz, t, g_norm, w, w2, ba, head_g):
    qk = HEADS * DK
    vv = HEADS * DV
    plan = tuple((False, ((n, s, hd),)) for n, s, hd in
                 ((2 * GLA_RANK, 1.0, GLA_GATE), (qk, DK ** -0.5, DK), (qk, 1.0, DK), (vv, 1.0, DV),
                  (vv, SILU_GATE, DV)))
    lg_hi, lg_lo, q, k, v, gate = _inproj(x, g_norm, w, plan, (BF16,) * 6, gate_weights=(w2, ba))
    return _gla_scan(q, k, v, lg_hi, lg_lo, bsz, t), gate, head_g.reshape(HEADS, 1, DV)


def _gate_rows(g):
    g4 = g.reshape((4, HEADS) + g.shape[1:])
    return jnp.concatenate([g4[0], g4[2], g4[1], g4[3]], axis=0)


def _split_cols(w, widths):
    out, s = [], 0
    for n in widths:
        out.append(w[:, s:s + n].astype(BF16))
        s += n
    return out


def _gla_gate_weights(w_a2, b_a):
    wf = w_a2[0].reshape(GLA_RANK, HEADS, DK)
    wb = w_a2[1].reshape(GLA_RANK, HEADS, DK)
    z = jnp.zeros_like(wf)
    w2 = jnp.concatenate([jnp.concatenate([wf, z], axis=2),
                          jnp.concatenate([z, wb], axis=2)], axis=0)
    ba = jnp.concatenate([b_a[0].reshape(HEADS, DK), b_a[1].reshape(HEADS, DK)], axis=1)
    return (w2.reshape(2 * GLA_RANK, HEADS * 2 * DK).astype(BF16),
            ba.reshape(1, HEADS * 2 * DK))


def kernel(x_prompt, x_sample, norm_mix_g, norm_ffn_g, norm_final_g, mlstm_w_in, mlstm_b_gate, mlstm_head_g, mlstm_w_out, gla_w_in, gla_w_a2, gla_b_a, gla_head_g, gla_w_out, ffn_w_gu, ffn_w_down):
    depth = norm_mix_g.shape[0]
    d_ff = ffn_w_down.shape[1]
    qk, vv = HEADS * DK, HEADS * DV
    mlstm_w = [_split_cols(w, (qk, qk, vv, vv, 4 * HEADS)) for w in mlstm_w_in]
    mlstm_w = [[wq, jnp.concatenate([wk.T, _gate_rows(wgt.T)], axis=0), wv, wo]
               for wq, wk, wv, wo, wgt in mlstm_w]
    gla_w = [_split_cols(w, (qk, qk, vv, vv, 2 * GLA_RANK)) for w in gla_w_in]
    gla_w = [[wa, wq, wk, wv, wr] for wq, wk, wv, wr, wa in gla_w]
    gla_gate = [_gla_gate_weights(w, b) for w, b in zip(gla_w_a2, gla_b_a)]
    w_out = [mlstm_w_out.astype(BF16), gla_w_out.astype(BF16)]
    w_gu = ffn_w_gu.astype(BF16)
    w_d = ffn_w_down.astype(BF16)

    outs = []
    for x0 in (x_prompt, x_sample):
        bsz, t, d = x0.shape
        x = x0.reshape(bsz * t, d)
        for i in range(depth):
            j = i // 2
            if i % 2 == 0:
                mixed = _mlstm_mixer(x, bsz, t, norm_mix_g[i], mlstm_w[j], mlstm_b_gate[j],
                                     mlstm_head_g[j])
            else:
                mixed = _gla_mixer(x, bsz, t, norm_mix_g[i], gla_w[j], *gla_gate[j], gla_head_g[j])
            x = _mix_ffn(*mixed, x, w_out[i % 2][j], norm_ffn_g[i], w_gu[i], w_d[i], norm_final_g,
                         final=(i == depth - 1))
        outs.append(x.reshape(bsz, t, d))
    return tuple(outs)
```
